```python
import math
import jax
import jax.numpy as jnp
from jax import lax
import numpy as np

D_MODEL = 4096
BATCH = 4
SEQ = 2048
DEPTH = 2
DEC_BATCH = 8
DEC_SEQ = 4
PAST_LEN = 16384
PAGE_SIZE = 128

HEAD_DIM = 128
MIX_WIDTH = D_MODEL
N_HEADS_TOTAL = MIX_WIDTH // HEAD_DIM
H_SB = (3 * N_HEADS_TOTAL) // 8
H_HG = N_HEADS_TOTAL // 4
H_DF = N_HEADS_TOTAL - H_SB - H_HG
HG_DK = 128
HG_DV = HEAD_DIM
DF_DQK = HEAD_DIM // 2
W_SB = H_SB * HEAD_DIM
W_HGK = H_HG * HG_DK
W_HG = H_HG * HG_DV
W_DF = H_DF * HEAD_DIM
D_FF = ((8 * D_MODEL + 3 * 256 - 1) // (3 * 256)) * 256
Q_BLOCK = 128
HG_CHUNK = 64
LN_EPS = 1e-5
RMS_EPS = 1e-6
MASK_NEG = -1e30
DEEPNORM_ALPHA = (2 * DEPTH) ** 0.25
DEEPNORM_BETA = (8 * DEPTH) ** -0.25
SB_SCALE = HEAD_DIM ** -0.5
DF_SCALE = DF_DQK ** -0.5

IN_SIZES = (W_SB, W_SB, W_SB, W_HGK, W_HGK, W_HG, W_HG, W_DF, W_DF, W_DF)
IN_IS_VALUE = (False, False, True, False, False, True, False, False, False, True)
IN_WIDTH = sum(IN_SIZES)
SPLIT_POINTS = [sum(IN_SIZES[:i + 1]) for i in range(len(IN_SIZES) - 1)]

kernel_name = 'hymba_stickbreak_hgrn2_diffattn_deepnorm_step'


def layer_norm(x, g, b):
    xf = x.astype(jnp.float32)
    mu = jnp.mean(xf, axis=-1, keepdims=True)
    var = jnp.mean(jnp.square(xf - mu), axis=-1, keepdims=True)
    y = (xf - mu) * lax.rsqrt(var + LN_EPS)
    return (y * g.astype(jnp.float32) + b.astype(jnp.float32)).astype(x.dtype)


def rms_norm(x, g):
    xf = x.astype(jnp.float32)
    y = xf * lax.rsqrt(jnp.mean(jnp.square(xf), axis=-1, keepdims=True) + RMS_EPS)
    return (y * g.astype(jnp.float32)).astype(x.dtype)


def swiglu(x, w_gate, w_up, w_down):
    h = jax.nn.silu(jnp.einsum('btd,df->btf', x, w_gate)) * jnp.einsum('btd,df->btf', x, w_up)
    return jnp.einsum('btf,fd->btd', h, w_down)


def over_query_blocks(fn, q, q_pos):
    B, T, H, Dq = q.shape
    if T <= Q_BLOCK or T % Q_BLOCK != 0:
        return fn(q, q_pos)
    n = T // Q_BLOCK
    qb = jnp.moveaxis(q.reshape(B, n, Q_BLOCK, H, Dq), 1, 0)
    pb = q_pos.reshape(n, Q_BLOCK)
    out = lax.map(lambda a: fn(a[0], a[1]), (qb, pb))
    return jnp.moveaxis(out, 0, 1).reshape(B, T, H, out.shape[-1])


def stick_breaking_attention(q, k, v, q_pos, k_pos):
    z = jnp.einsum('bthd,bshd->bhts', q, k).astype(jnp.float32) * SB_SCALE
    mask = k_pos[None, :] < q_pos[:, None]
    log_keep = jnp.where(mask, jax.nn.log_sigmoid(-z), 0.0)
    between = lax.cumsum(log_keep, axis=3, reverse=True) - log_keep
    log_w = jnp.where(mask, jax.nn.log_sigmoid(z) + between, MASK_NEG)
    w = jnp.exp(log_w)
    return jnp.einsum('bhts,bshd->bthd', w.astype(v.dtype), v)


def differential_attention(q, k, v, q_pos, k_pos, lam):
    q1, q2 = jnp.split(q, 2, axis=-1)
    k1, k2 = jnp.split(k, 2, axis=-1)
    mask = k_pos[None, :] <= q_pos[:, None]

    def causal_softmax(qa, ka):
        s = jnp.einsum('bthd,bshd->bhts', qa, ka).astype(jnp.float32) * DF_SCALE
        return jax.nn.softmax(jnp.where(mask, s, MASK_NEG), axis=-1)

    w = causal_softmax(q1, k1) - lam * causal_softmax(q2, k2)
    return jnp.einsum('bhts,bshd->bthd', w.astype(v.dtype), v)


def hgrn2_recurrence(q, k, v, log_f, s0, chunk):
    B, T, H, DK = q.shape
    n = T // chunk
    tri = jnp.tril(jnp.ones((chunk, chunk), dtype=bool))

    def to_chunks(a):
        return jnp.moveaxis(a.reshape(B, n, chunk, H, a.shape[-1]), 1, 0)

    def step(S, inp):
        qc, kc, vc, gc = inp
        b = jnp.cumsum(gc, axis=1)
        o_inter = jnp.einsum('bthk,bhkv->bthv', qc * jnp.exp(b), S)
        diff = b[:, :, None] - b[:, None, :]
        decay = jnp.exp(jnp.where(tri[None, :, :, None, None], diff, MASK_NEG))
        att = jnp.einsum('bthk,bshk,btshk->bhts', qc, kc, decay)
        o_intra = jnp.einsum('bhts,bshv->bthv', att, vc)
        b_last = b[:, -1]
        S_new = jnp.exp(b_last)[..., None] * S + jnp.einsum(
            'bshk,bshv->bhkv', kc * jnp.exp(b_last[:, None] - b), vc)
        return S_new, o_inter + o_intra

    S, o = lax.scan(step, s0, (to_chunks(q), to_chunks(k), to_chunks(v), to_chunks(log_f)))
    return jnp.moveaxis(o, 0, 1).reshape(B, T, H, v.shape[-1]), S


def gather_pages(pool, layer, page_table):
    rows = pool[layer, page_table]
    return rows.reshape(page_table.shape[0], -1, pool.shape[-2], pool.shape[-1])


def decoder_layer(x, layer, lower_bound, k_sb_past, v_sb_past, k_df_past, v_df_past, s_hg_past,
                  w_in, w_out, sb_norm_g, hg_norm_g, df_lq1, df_lk1, df_lq2, df_lk2, df_norm_g,
                  ln1_g, ln1_b, w_gate, w_up, w_down, ln2_g, ln2_b):
    B, T, _ = x.shape
    past_len = k_sb_past.shape[1]
    proj = jnp.einsum('btd,de->bte', x, w_in[layer])
    sb_q, sb_k, sb_v, hg_q, hg_f, hg_i, hg_g, df_q, df_k, df_v = jnp.split(proj, SPLIT_POINTS, axis=-1)
    q_pos = past_len + jnp.arange(T, dtype=jnp.int32)
    k_pos = jnp.arange(past_len + T, dtype=jnp.int32)

    sb_q = sb_q.reshape(B, T, H_SB, HEAD_DIM)
    sb_k = sb_k.reshape(B, T, H_SB, HEAD_DIM)
    sb_v = sb_v.reshape(B, T, H_SB, HEAD_DIM)
    sb_k_all = jnp.concatenate([k_sb_past, sb_k], axis=1)
    sb_v_all = jnp.concatenate([v_sb_past, sb_v], axis=1)
    o_sb = over_query_blocks(
        lambda qb, pb: stick_breaking_attention(qb, sb_k_all, sb_v_all, pb, k_pos), sb_q, q_pos)
    o_sb = rms_norm(o_sb, sb_norm_g[layer])

    lb = lower_bound.reshape(H_HG, HG_DK)
    f_pre = hg_f.astype(jnp.float32).reshape(B, T, H_HG, HG_DK)
    log_f = jax.nn.log_sigmoid(f_pre) + jnp.log1p(lb * jnp.exp(-f_pre))
    hg_key = (1.0 - lb) * jax.nn.sigmoid(-f_pre)
    hg_qa = jax.nn.silu(hg_q.astype(jnp.float32)).reshape(B, T, H_HG, HG_DK)
    hg_va = hg_i.astype(jnp.float32).reshape(B, T, H_HG, HG_DV)
    chunk = HG_CHUNK if T % HG_CHUNK == 0 else T
    o_hg, s_hg = hgrn2_recurrence(hg_qa, hg_key, hg_va, log_f, s_hg_past.astype(jnp.float32), chunk)
    o_hg = rms_norm(o_hg.astype(x.dtype), hg_norm_g[layer]) * jax.nn.silu(hg_g.reshape(B, T, H_HG, HG_DV))

    lam_init = 0.8 - 0.6 * math.exp(-0.3 * layer)
    lam = (jnp.exp(jnp.sum(df_lq1[layer].astype(jnp.float32) * df_lk1[layer].astype(jnp.float32)))
           - jnp.exp(jnp.sum(df_lq2[layer].astype(jnp.float32) * df_lk2[layer].astype(jnp.float32)))
           + lam_init)
    df_q = df_q.reshape(B, T, H_DF, HEAD_DIM)
    df_k = df_k.reshape(B, T, H_DF, HEAD_DIM)
    df_v = df_v.reshape(B, T, H_DF, HEAD_DIM)
    df_k_all = jnp.concatenate([k_df_past, df_k], axis=1)
    df_v_all = jnp.concatenate([v_df_past, df_v], axis=1)
    o_df = over_query_blocks(
        lambda qb, pb: differential_attention(qb, df_k_all, df_v_all, pb, k_pos, lam), df_q, q_pos)
    o_df = rms_norm(o_df, df_norm_g[layer]) * (1.0 - lam_init)

    o = jnp.concatenate([o_sb.reshape(B, T, -1), o_hg.reshape(B, T, -1), o_df.reshape(B, T, -1)], axis=-1)
    m = jnp.einsum('bte,ed->btd', o, w_out[layer])
    h = layer_norm(DEEPNORM_ALPHA * x + m, ln1_g[layer], ln1_b[layer])
    y = layer_norm(DEEPNORM_ALPHA * h + swiglu(h, w_gate[layer], w_up[layer], w_down[layer]),
                   ln2_g[layer], ln2_b[layer])
    return y, sb_k, sb_v, df_k, df_v, s_hg.astype(s_hg_past.dtype)


def setup_inputs(seed: int = 0) -> dict:
    key = jax.random.key(seed)
    ks = jax.random.split(key, 32)
    f32 = jnp.float32
    n_pages = PAST_LEN // PAGE_SIZE
    n_pool = (DEC_BATCH * n_pages * 5) // 4

    def nrm(k, shape, s):
        return jax.random.normal(k, shape, f32) * s

    x_prompt = nrm(ks[0], (BATCH, SEQ, D_MODEL), 1.0)
    x_sample = nrm(ks[1], (DEC_BATCH, DEC_SEQ, D_MODEL), 1.0)
    cache_sb_k = nrm(ks[2], (DEPTH, n_pool, PAGE_SIZE, H_SB, HEAD_DIM), 1.0)
    cache_sb_v = nrm(ks[3], (DEPTH, n_pool, PAGE_SIZE, H_SB, HEAD_DIM), DEEPNORM_BETA)
    cache_df_k = nrm(ks[4], (DEPTH, n_pool, PAGE_SIZE, H_DF, HEAD_DIM), 1.0)
    cache_df_v = nrm(ks[5], (DEPTH, n_pool, PAGE_SIZE, H_DF, HEAD_DIM), DEEPNORM_BETA)
    state_hg = nrm(ks[6], (DEPTH, DEC_BATCH, H_HG, HG_DK, HG_DV), 0.5)
    perm = jax.random.permutation(ks[7], n_pool)
    page_table = perm[: DEC_BATCH * n_pages].reshape(DEC_BATCH, n_pages).astype(jnp.int32)

    col_scale = jnp.concatenate(
        [jnp.full((n,), DEEPNORM_BETA if is_v else 1.0, f32) for n, is_v in zip(IN_SIZES, IN_IS_VALUE)])
    w_in = nrm(ks[8], (DEPTH, D_MODEL, IN_WIDTH), D_MODEL ** -0.5) * col_scale
    w_out = nrm(ks[9], (DEPTH, MIX_WIDTH, D_MODEL), MIX_WIDTH ** -0.5 * DEEPNORM_BETA)
    sb_norm_g = 1.0 + nrm(ks[10], (DEPTH, HEAD_DIM), 0.02)
    hg_lb = nrm(ks[11], (DEPTH, W_HGK), 0.1)
    hg_norm_g = 1.0 + nrm(ks[12], (DEPTH, HG_DV), 0.02)
    df_lq1 = nrm(ks[13], (DEPTH, DF_DQK), 0.1)
    df_lk1 = nrm(ks[14], (DEPTH, DF_DQK), 0.1)
    df_lq2 = nrm(ks[15], (DEPTH, DF_DQK), 0.1)
    df_lk2 = nrm(ks[16], (DEPTH, DF_DQK), 0.1)
    df_norm_g = 1.0 + nrm(ks[17], (DEPTH, HEAD_DIM), 0.02)
    ln1_g = 1.0 + nrm(ks[18], (DEPTH, D_MODEL), 0.02)
    ln1_b = nrm(ks[19], (DEPTH, D_MODEL), 0.02)
    w_gate = nrm(ks[20], (DEPTH, D_MODEL, D_FF), D_MODEL ** -0.5)
    w_up = nrm(ks[21], (DEPTH, D_MODEL, D_FF), D_MODEL ** -0.5)
    w_down = nrm(ks[22], (DEPTH, D_FF, D_MODEL), D_FF ** -0.5 * DEEPNORM_BETA)
    ln2_g = 1.0 + nrm(ks[23], (DEPTH, D_MODEL), 0.02)
    ln2_b = nrm(ks[24], (DEPTH, D_MODEL), 0.02)
    return {'x_prompt': x_prompt, 'x_sample': x_sample,
            'cache_sb_k': cache_sb_k, 'cache_sb_v': cache_sb_v,
            'cache_df_k': cache_df_k, 'cache_df_v': cache_df_v,
            'state_hg': state_hg, 'page_table': page_table,
            'w_in': w_in, 'w_out': w_out, 'sb_norm_g': sb_norm_g, 'hg_lb': hg_lb,
            'hg_norm_g': hg_norm_g, 'df_lq1': df_lq1, 'df_lk1': df_lk1, 'df_lq2': df_lq2,
            'df_lk2': df_lk2, 'df_norm_g': df_norm_g, 'ln1_g': ln1_g, 'ln1_b': ln1_b,
            'w_gate': w_gate, 'w_up': w_up, 'w_down': w_down, 'ln2_g': ln2_g, 'ln2_b': ln2_b}


def reference(x_prompt, x_sample, cache_sb_k, cache_sb_v, cache_df_k, cache_df_v, state_hg, page_table,
              w_in, w_out, sb_norm_g, hg_lb, hg_norm_g, df_lq1, df_lk1, df_lq2, df_lk2, df_norm_g,
              ln1_g, ln1_b, w_gate, w_up, w_down, ln2_g, ln2_b):
    params = (w_in, w_out, sb_norm_g, hg_norm_g, df_lq1, df_lk1, df_lq2, df_lk2, df_norm_g,
              ln1_g, ln1_b, w_gate, w_up, w_down, ln2_g, ln2_b)
    lb_w = jax.nn.softmax(hg_lb.astype(jnp.float32), axis=0)
    lower_bounds = jnp.cumsum(lb_w, axis=0) - lb_w[0:1]
    bp = x_prompt.shape[0]
    hp = x_prompt
    hs = x_sample
    prompt_new = ([], [], [], [], [])
    sample_new = ([], [], [], [], [])
    for layer in range(DEPTH):
        empty_sb = jnp.zeros((bp, 0, H_SB, HEAD_DIM), hp.dtype)
        empty_df = jnp.zeros((bp, 0, H_DF, HEAD_DIM), hp.dtype)
        s_zero = jnp.zeros((bp, H_HG, HG_DK, HG_DV), state_hg.dtype)
        hp, *new_p = decoder_layer(hp, layer, lower_bounds[layer], empty_sb, empty_sb, empty_df, empty_df,
                                   s_zero, *params)
        hs, *new_s = decoder_layer(hs, layer, lower_bounds[layer],
                                   gather_pages(cache_sb_k, layer, page_table),
                                   gather_pages(cache_sb_v, layer, page_table),
                                   gather_pages(cache_df_k, layer, page_table),
                                   gather_pages(cache_df_v, layer, page_table),
                                   state_hg[layer], *params)
        for lst, a in zip(prompt_new, new_p):
            lst.append(a)
        for lst, a in zip(sample_new, new_s):
            lst.append(a)
    new_sb_k_prompt = jnp.stack(prompt_new[0])
    new_sb_v_prompt = jnp.stack(prompt_new[1])
    new_df_k_prompt = jnp.stack(prompt_new[2])
    new_df_v_prompt = jnp.stack(prompt_new[3])
    new_state_hg_prompt = jnp.stack(prompt_new[4])
    new_sb_k_sample = jnp.stack(sample_new[0])
    new_sb_v_sample = jnp.stack(sample_new[1])
    new_df_k_sample = jnp.stack(sample_new[2])
    new_df_v_sample = jnp.stack(sample_new[3])
    new_state_hg_sample = jnp.stack(sample_new[4])
    y_prompt = hp
    y_sample = hs
    return (y_prompt, y_sample,
            new_sb_k_prompt, new_sb_v_prompt, new_df_k_prompt, new_df_v_prompt, new_state_hg_prompt,
            new_sb_k_sample, new_sb_v_sample, new_df_k_sample, new_df_v_sample, new_state_hg_sample)
```

```python
import functools
import math

import jax
import jax.numpy as jnp
from jax import lax
from jax.experimental import pallas as pl
from jax.experimental.pallas import tpu as pltpu

F32 = jnp.float32
BF16 = jnp.bfloat16

LANE = 128
SUBLANE = 8
VMEM_CAP = 60000 * 1024
HG_CHUNK = 16
Q_SLOTS = SUBLANE
LN_EPS = 1e-5
RMS_EPS = 1e-6
MASK_NEG = -1e30

_NT = (((1,), (1,)), ((), ()))
_TN = (((0,), (0,)), ((), ()))


def _params(n_axes, vmem_bytes):
    limit = int(min(VMEM_CAP, max(32 * 1024 * 1024, vmem_bytes)))
    return pltpu.CompilerParams(dimension_semantics=("arbitrary",) * n_axes, vmem_limit_bytes=limit)


def _pick(dim, pref, align):
    if dim <= pref:
        return dim
    t = (pref // align) * align
    while t >= align:
        if dim % t == 0:
            return t
        t -= align
    return dim


def _nbytes(shape, dtype):
    return math.prod(shape) * jnp.dtype(dtype).itemsize


def _mm_body(x_ref, w_ref, o_ref, *acc, nk):
    part = jnp.dot(x_ref[...], w_ref[...], preferred_element_type=F32)
    if nk == 1:
        o_ref[...] = part.astype(o_ref.dtype)
        return
    acc_ref, = acc
    k = pl.program_id(2)

    @pl.when(k == 0)
    def _():
        acc_ref[...] = part

    @pl.when(k > 0)
    def _():
        acc_ref[...] += part

    @pl.when(k == nk - 1)
    def _():
        o_ref[...] = acc_ref[...].astype(o_ref.dtype)


def _matmul(x, w, out_dtype, tm, tn, tk, name):
    M, K = x.shape
    N = w.shape[1]
    tm, tn, tk = _pick(M, tm, SUBLANE), _pick(N, tn, LANE), _pick(K, tk, LANE)
    nk = K // tk
    vmem = 2 * (_nbytes((tm, tk), BF16) + _nbytes((tk, tn), BF16) + _nbytes((tm, tn), out_dtype))
    vmem += 3 * _nbytes((tm, tn), F32)
    return pl.pallas_call(
        functools.partial(_mm_body, nk=nk),
        grid=(M // tm, N // tn, nk),
        in_specs=[pl.BlockSpec((tm, tk), lambda i, j, k: (i, k)),
                  pl.BlockSpec((tk, tn), lambda i, j, k: (k, j))],
        out_specs=pl.BlockSpec((tm, tn), lambda i, j, k: (i, j)),
        out_shape=jax.ShapeDtypeStruct((M, N), out_dtype),
        scratch_shapes=[pltpu.VMEM((tm, tn), F32)] if nk > 1 else [],
        compiler_params=_params(3, vmem),
        name=name,
    )(x, w)


def _sigmoid(x):
    return 1.0 / (1.0 + jnp.exp(-x))


def _gateup_body(x_ref, wg_ref, wu_ref, o_ref):
    x = x_ref[...]
    g = jnp.dot(x, wg_ref[...], preferred_element_type=F32)
    u = jnp.dot(x, wu_ref[...], preferred_element_type=F32)
    o_ref[...] = (g * _sigmoid(g) * u).astype(o_ref.dtype)


def _gateup(x, wg, wu, tm, tn):
    M, K = x.shape
    N = wg.shape[1]
    tm, tn = _pick(M, tm, SUBLANE), _pick(N, tn, LANE)
    vmem = 2 * (_nbytes((tm, K), BF16) + 2 * _nbytes((K, tn), BF16) + _nbytes((tm, tn), BF16))
    vmem += 4 * _nbytes((tm, tn), F32)
    return pl.pallas_call(
        _gateup_body,
        grid=(M // tm, N // tn),
        in_specs=[pl.BlockSpec((tm, K), lambda i, j: (i, 0)),
                  pl.BlockSpec((K, tn), lambda i, j: (0, j)),
                  pl.BlockSpec((K, tn), lambda i, j: (0, j))],
        out_specs=pl.BlockSpec((tm, tn), lambda i, j: (i, j)),
        out_shape=jax.ShapeDtypeStruct((M, N), BF16),
        compiler_params=_params(2, vmem),
        name="swiglu_gate_up",
    )(x, wg, wu)


def _ln_body(x_ref, m_ref, g_ref, b_ref, o_ref, ob_ref, *, alpha):
    v = alpha * x_ref[...] + m_ref[...]
    mu = jnp.mean(v, axis=-1, keepdims=True)
    d = v - mu
    var = jnp.mean(d * d, axis=-1, keepdims=True)
    y = d * lax.rsqrt(var + LN_EPS) * g_ref[...] + b_ref[...]
    o_ref[...] = y
    ob_ref[...] = y.astype(BF16)


def _residual_layernorm(x, m, g, b, alpha, name):
    M, D = x.shape
    tm = _pick(M, 256, SUBLANE)
    vmem = 2 * (3 * _nbytes((tm, D), F32) + _nbytes((tm, D), BF16)) + 4 * _nbytes((tm, D), F32)
    row = pl.BlockSpec((tm, D), lambda i: (i, 0))
    vec = pl.BlockSpec((1, D), lambda i: (0, 0))
    return pl.pallas_call(
        functools.partial(_ln_body, alpha=alpha),
        grid=(M // tm,),
        in_specs=[row, row, vec, vec],
        out_specs=[row, row],
        out_shape=[jax.ShapeDtypeStruct((M, D), F32), jax.ShapeDtypeStruct((M, D), BF16)],
        compiler_params=_params(1, vmem),
        name=name,
    )(x, m, g, b)


def _softplus(z):
    return jnp.maximum(z, 0.0) + jnp.log1p(jnp.exp(-jnp.abs(z)))


def _rms(o, g):
    return o * lax.rsqrt(jnp.mean(o * o, axis=-1, keepdims=True) + RMS_EPS) * g


def _lanes(x, n):
    reps = n // LANE
    return x if reps == 1 else jnp.concatenate([x] * reps, axis=1)


def _cumsum_weights():
    j = jnp.arange(LANE)[:, None]
    s = jnp.arange(LANE)[None, :]
    blk = jnp.concatenate([(j > s).astype(BF16), jnp.ones((LANE, LANE), BF16)], axis=1)
    return jnp.concatenate([blk, blk], axis=0)


def _sb_weights(z, valid, run, wcs):
    sp = _softplus(z)
    log_keep = -sp if valid is None else jnp.where(valid, -sp, 0.0)
    log_beta = z - sp
    groups = z.shape[1] // LANE
    ws = [None] * groups
    for g in reversed(range(groups)):
        sl = slice(g * LANE, (g + 1) * LANE)
        lk = log_keep[:, sl]
        hi = lk.astype(BF16)
        lo = (lk - hi.astype(F32)).astype(BF16)
        cs = jnp.dot(jnp.concatenate([hi, lo], axis=1), wcs, preferred_element_type=F32)
        ws[g] = jnp.exp(log_beta[:, sl] + cs[:, :LANE] + run)
        run = run + cs[:, LANE:]
    w = ws[0] if groups == 1 else jnp.concatenate(ws, axis=1)
    if valid is not None:
        w = jnp.where(valid, w, 0.0)
    return w, run


def _softmax_step(s, vt, m_ref, l_ref, acc_ref):
    m_prev = m_ref[...]
    m_new = jnp.maximum(m_prev, jnp.max(s, axis=1, keepdims=True))
    alpha = jnp.exp(m_prev - m_new)
    p = jnp.exp(s - _lanes(m_new, s.shape[1]))
    l_ref[...] = alpha * l_ref[...] + jnp.sum(p, axis=1, keepdims=True)
    acc_ref[...] = acc_ref[...] * _lanes(alpha, acc_ref.shape[1]) + jnp.dot(
        p.astype(BF16), vt, preferred_element_type=F32)
    m_ref[...] = m_new


def _df_lambda(lq1_ref, lk1_ref, lq2_ref, lk2_ref, lam_init):
    s1 = jnp.sum(lq1_ref[...] * lk1_ref[...], axis=1, keepdims=True)
    s2 = jnp.sum(lq2_ref[...] * lk2_ref[...], axis=1, keepdims=True)
    return jnp.exp(s1) - jnp.exp(s2) + lam_init


def _sb_prompt_body(q_ref, k_ref, v_ref, g_ref, wcs_ref, o_ref, kb_ref, vb_ref, acc_ref, run_ref,
                    *, tq, scale):
    i = pl.program_id(2)

    @pl.when(i == 0)
    def _():
        kb_ref[...] = k_ref[...].astype(BF16)
        vb_ref[...] = v_ref[...].astype(BF16)

    qb = q_ref[...].astype(BF16)
    wcs = wcs_ref[...]
    acc_ref[...] = jnp.zeros_like(acc_ref)
    run_ref[...] = jnp.zeros_like(run_ref)

    def tile(j, valid):
        ks = pl.multiple_of(j * tq, tq)
        kt = kb_ref[pl.ds(ks, tq), :]
        vt = vb_ref[pl.ds(ks, tq), :]
        z = lax.dot_general(qb, kt, _NT, preferred_element_type=F32) * scale
        w, run = _sb_weights(z, valid, run_ref[...], wcs)
        run_ref[...] = run
        acc_ref[...] += jnp.dot(w.astype(BF16), vt, preferred_element_type=F32)

    row = lax.broadcasted_iota(jnp.int32, (tq, tq), 0)
    col = lax.broadcasted_iota(jnp.int32, (tq, tq), 1)
    tile(i, col < row)

    def body(jj, c):
        tile(i - 1 - jj, None)
        return c

    lax.fori_loop(0, i, body, 0)
    o_ref[...] = _rms(acc_ref[...], g_ref[...]).astype(o_ref.dtype)


def _df_prompt_body(q_ref, k_ref, v_ref, g_ref, lq1_ref, lk1_ref, lq2_ref, lk2_ref, o_ref,
                    kb_ref, vb_ref, m_ref, l_ref, acc_ref, *, tq, scale, lam_init):
    i = pl.program_id(2)

    @pl.when(i == 0)
    def _():
        kb_ref[...] = k_ref[...].astype(BF16)
        vb_ref[...] = v_ref[...].astype(BF16)

    q = q_ref[...]
    lane = lax.broadcasted_iota(jnp.int32, q.shape, 1)
    first = lane < LANE // 2
    qq = jnp.concatenate([jnp.where(first, q, 0.0), jnp.where(first, 0.0, q)], axis=0).astype(BF16)
    m_ref[...] = jnp.full_like(m_ref, MASK_NEG)
    l_ref[...] = jnp.zeros_like(l_ref)
    acc_ref[...] = jnp.zeros_like(acc_ref)

    def tile(j, valid):
        ks = pl.multiple_of(j * tq, tq)
        kt = kb_ref[pl.ds(ks, tq), :]
        vt = vb_ref[pl.ds(ks, tq), :]
        s = lax.dot_general(qq, kt, _NT, preferred_element_type=F32) * scale
        if valid is not None:
            s = jnp.where(valid, s, MASK_NEG)
        _softmax_step(s, vt, m_ref, l_ref, acc_ref)

    row = lax.broadcasted_iota(jnp.int32, (2 * tq, tq), 0)
    col = lax.broadcasted_iota(jnp.int32, (2 * tq, tq), 1)
    row = jnp.where(row >= tq, row - tq, row)
    tile(i, col <= row)

    def body(jj, c):
        tile(jj, None)
        return c

    lax.fori_loop(0, i, body, 0)
    o = acc_ref[...] / l_ref[...]
    lam = _df_lambda(lq1_ref, lk1_ref, lq2_ref, lk2_ref, lam_init)
    o = o[:tq] - lam * o[tq:]
    o_ref[...] = (_rms(o, g_ref[...]) * (1.0 - lam_init)).astype(o_ref.dtype)


def _prompt_attention(kind, proj, batch, seq, heads, col0, norm_g, extra, tq):
    tq = _pick(seq, tq, LANE)
    nq = seq // tq
    qspec = pl.BlockSpec((tq, LANE), lambda b, h, i: (b * nq + i, col0 + h))
    kspec = pl.BlockSpec((seq, LANE), lambda b, h, i: (b, col0 + heads + h))
    vspec = pl.BlockSpec((seq, LANE), lambda b, h, i: (b, col0 + 2 * heads + h))
    const = lambda shape: pl.BlockSpec(shape, lambda b, h, i: (0,) * len(shape))
    kv_scratch = [pltpu.VMEM((seq, LANE), BF16), pltpu.VMEM((seq, LANE), BF16)]
    vmem = 2 * (2 * _nbytes((seq, LANE), F32) + 2 * _nbytes((tq, LANE), F32)) + 2 * _nbytes((seq, LANE), BF16)
    vmem += 24 * _nbytes((2 * tq, tq), F32)
    if kind == "sb":
        body = functools.partial(_sb_prompt_body, tq=tq, scale=LANE ** -0.5)
        ins = [proj, proj, proj, norm_g, _cumsum_weights()]
        specs = [qspec, kspec, vspec, const((1, LANE)), const((2 * LANE, 2 * LANE))]
        scratch = kv_scratch + [pltpu.VMEM((tq, LANE), F32), pltpu.VMEM((tq, LANE), F32)]
    else:
        lam_init, lq1, lk1, lq2, lk2 = extra
        body = functools.partial(_df_prompt_body, tq=tq, scale=(LANE // 2) ** -0.5, lam_init=lam_init)
        ins = [proj, proj, proj, norm_g, lq1, lk1, lq2, lk2]
        specs = [qspec, kspec, vspec, const((1, LANE))] + [const((1, LANE // 2))] * 4
        scratch = kv_scratch + [pltpu.VMEM((2 * tq, LANE), F32)] * 3
    return pl.pallas_call(
        body,
        grid=(batch, heads, nq),
        in_specs=specs,
        out_specs=pl.BlockSpec((tq, LANE), lambda b, h, i: (b * nq + i, h)),
        out_shape=jax.ShapeDtypeStruct((batch * seq, heads * LANE), BF16),
        scratch_shapes=scratch,
        compiler_params=_params(3, vmem),
        name=kind + "_attention_prompt",
    )(*ins)


def _hgrn_body(q_ref, f_ref, i_ref, gate_ref, lbp_ref, gn_ref, lmat_ref, s0_ref, o_ref, s_ref,
               st_ref, qa_ref, key_ref, bl_ref, oraw_ref, *, rows, layer, depth, t_valid, nt):
    t = pl.program_id(2)

    @pl.when(t == 0)
    def _():
        st_ref[...] = s0_ref[...].T

    lbp = lbp_ref[...]
    prow = [lbp[r:r + 1] for r in range(depth)]
    mx = functools.reduce(jnp.maximum, prow)
    es = [jnp.exp(r - mx) for r in prow]
    lb = sum(es[1:layer + 1], jnp.zeros_like(mx)) / sum(es[1:], es[0])

    p = f_ref[...]
    log_f = -_softplus(-p) + jnp.log1p(lb * jnp.exp(-p))
    key = (1.0 - lb) * (1.0 / (1.0 + jnp.exp(p)))
    qv = q_ref[...]
    if t_valid is not None:
        ok = (t * rows + lax.broadcasted_iota(jnp.int32, p.shape, 0)) < t_valid
        log_f = jnp.where(ok, log_f, 0.0)
        key = jnp.where(ok, key, 0.0)
    hi = log_f.astype(BF16)
    r1 = log_f - hi.astype(F32)
    mid = r1.astype(BF16)
    lo = (r1 - mid.astype(F32)).astype(BF16)
    bl_ref[...] = jnp.dot(lmat_ref[...], jnp.concatenate([hi, mid, lo], axis=0), preferred_element_type=F32)
    qa_ref[...] = qv * _sigmoid(qv)
    key_ref[...] = key

    rowi = lax.broadcasted_iota(jnp.int32, (HG_CHUNK, LANE), 0)

    def chunk(c, carry):
        r0 = pl.multiple_of(c * HG_CHUNK, HG_CHUNK)
        blc = bl_ref[pl.ds(r0, HG_CHUNK), :]
        qac = qa_ref[pl.ds(r0, HG_CHUNK), :]
        kc = key_ref[pl.ds(r0, HG_CHUNK), :]
        vc = i_ref[pl.ds(r0, HG_CHUNK), :]
        blast = blc[HG_CHUNK - 1:HG_CHUNK]
        st = st_ref[...]
        o = lax.dot_general((qac * jnp.exp(blc)).astype(BF16), st.astype(BF16), _NT,
                            preferred_element_type=F32)
        for s in range(HG_CHUNK):
            e = jnp.exp(jnp.where(rowi >= s, blc - blc[s:s + 1], MASK_NEG))
            cv = jnp.sum(qac * e * kc[s:s + 1], axis=1, keepdims=True)
            o = o + cv * vc[s:s + 1]
        oraw_ref[pl.ds(r0, HG_CHUNK), :] = o
        kt = kc * jnp.exp(blast - blc)
        upd = lax.dot_general(vc.astype(BF16), kt.astype(BF16), _TN, preferred_element_type=F32)
        st_ref[...] = st * jnp.exp(blast) + upd
        return carry

    lax.fori_loop(0, rows // HG_CHUNK, chunk, 0)
    gate = gate_ref[...]
    o_ref[...] = (_rms(oraw_ref[...], gn_ref[...]) * (gate * _sigmoid(gate))).astype(o_ref.dtype)

    @pl.when(t == nt - 1)
    def _():
        s_ref[...] = st_ref[...].T


def _hgrn(proj, batch, seq, heads, col0, hg_lb, norm_g, s0, layer, t_valid):
    rows = _pick(seq, 256, HG_CHUNK)
    nt = seq // rows
    depth = hg_lb.shape[0]
    r = jnp.arange(rows)
    lmat = ((r[:, None] // HG_CHUNK == r[None, :] // HG_CHUNK) & (r[None, :] <= r[:, None])).astype(BF16)
    lmat = jnp.concatenate([lmat] * 3, axis=1)

    def col(k):
        return pl.BlockSpec((rows, LANE), lambda b, h, t: (b * nt + t, col0 + k * heads + h))

    state = pl.BlockSpec((None, None, LANE, LANE), lambda b, h, t: (b, h, 0, 0))
    vmem = 2 * (5 * _nbytes((rows, LANE), F32) + 4 * _nbytes((LANE, LANE), F32) + _nbytes((rows, 3 * rows), BF16))
    vmem += 12 * _nbytes((rows, LANE), F32)
    return pl.pallas_call(
        functools.partial(_hgrn_body, rows=rows, layer=layer, depth=depth,
                          t_valid=None if t_valid == seq else t_valid, nt=nt),
        grid=(batch, heads, nt),
        in_specs=[col(0), col(1), col(2), col(3),
                  pl.BlockSpec((depth, LANE), lambda b, h, t: (0, h)),
                  pl.BlockSpec((1, LANE), lambda b, h, t: (0, 0)),
                  pl.BlockSpec((rows, 3 * rows), lambda b, h, t: (0, 0)),
                  state],
        out_specs=[pl.BlockSpec((rows, LANE), lambda b, h, t: (b * nt + t, h)), state],
        out_shape=[jax.ShapeDtypeStruct((batch * seq, heads * LANE), BF16),
                   jax.ShapeDtypeStruct(s0.shape, F32)],
        scratch_shapes=[pltpu.VMEM((LANE, LANE), F32)] + [pltpu.VMEM((rows, LANE), F32)] * 4,
        compiler_params=_params(3, vmem),
        name="hgrn2",
    )(proj, proj, proj, proj, hg_lb, norm_g, lmat, s0)


def _write_heads(o_ref, head_fn, heads):
    pad = jnp.zeros((o_ref.shape[0] - Q_SLOTS, LANE), F32)
    for h in range(heads):
        o_ref[:, h * LANE:(h + 1) * LANE] = jnp.concatenate([head_fn(h), pad], axis=0).astype(o_ref.dtype)


def _sb_decode_body(pt_ref, q_ref, kn_ref, vn_ref, *rest, group, heads, scale):
    k_refs, v_refs = rest[:group], rest[group:2 * group]
    g_ref, wcs_ref, o_ref, acc_ref, run_ref = rest[2 * group:]
    p = pl.program_id(1)
    qb = q_ref[...]
    wcs = wcs_ref[...]
    rows = qb.shape[0]

    @pl.when(p == 0)
    def _():
        kb = kn_ref[...].astype(BF16)
        z = lax.dot_general(qb, kb, _NT, preferred_element_type=F32) * scale
        tok = lax.broadcasted_iota(jnp.int32, z.shape, 0) & (Q_SLOTS - 1)
        col = lax.broadcasted_iota(jnp.int32, z.shape, 1)
        w, run = _sb_weights(z, col < tok, jnp.zeros((rows, LANE), F32), wcs)
        run_ref[...] = run
        acc_ref[...] = jnp.dot(w.astype(BF16), vn_ref[...].astype(BF16), preferred_element_type=F32)

    @pl.when(p > 0)
    def _():
        kb = jnp.concatenate([r[...].astype(BF16) for r in k_refs], axis=0)
        vb = jnp.concatenate([r[...].astype(BF16) for r in v_refs], axis=0)
        z = lax.dot_general(qb, kb, _NT, preferred_element_type=F32) * scale
        w, run = _sb_weights(z, None, run_ref[...], wcs)
        run_ref[...] = run
        acc_ref[...] += jnp.dot(w.astype(BF16), vb, preferred_element_type=F32)

    @pl.when(p == pl.num_programs(1) - 1)
    def _():
        g = g_ref[...]
        _write_heads(o_ref, lambda h: _rms(
            acc_ref[h * Q_SLOTS:(h + 1) * Q_SLOTS, h * LANE:(h + 1) * LANE], g), heads)


def _df_decode_body(pt_ref, q_ref, kn_ref, vn_ref, *rest, group, heads, scale, lam_init):
    k_refs, v_refs = rest[:group], rest[group:2 * group]
    g_ref, lq1_ref, lk1_ref, lq2_ref, lk2_ref, o_ref, m_ref, l_ref, acc_ref = rest[2 * group:]
    p = pl.program_id(1)
    qb = q_ref[...]
    half = qb.shape[0] // 2

    @pl.when(p == 0)
    def _():
        m_ref[...] = jnp.full_like(m_ref, MASK_NEG)
        l_ref[...] = jnp.zeros_like(l_ref)
        acc_ref[...] = jnp.zeros_like(acc_ref)
        kb = kn_ref[...].astype(BF16)
        s = lax.dot_general(qb, kb, _NT, preferred_element_type=F32) * scale
        tok = lax.broadcasted_iota(jnp.int32, s.shape, 0) & (Q_SLOTS - 1)
        col = lax.broadcasted_iota(jnp.int32, s.shape, 1)
        s = jnp.where(col <= tok, s, MASK_NEG)
        _softmax_step(s, vn_ref[...].astype(BF16), m_ref, l_ref, acc_ref)

    @pl.when(p > 0)
    def _():
        kb = jnp.concatenate([r[...].astype(BF16) for r in k_refs], axis=0)
        vb = jnp.concatenate([r[...].astype(BF16) for r in v_refs], axis=0)
        s = lax.dot_general(qb, kb, _NT, preferred_element_type=F32) * scale
        _softmax_step(s, vb, m_ref, l_ref, acc_ref)

    @pl.when(p == pl.num_programs(1) - 1)
    def _():
        g = g_ref[...]
        lam = _df_lambda(lq1_ref, lk1_ref, lq2_ref, lk2_ref, lam_init)

        def head(h):
            r1 = slice(h * Q_SLOTS, (h + 1) * Q_SLOTS)
            r2 = slice(half + h * Q_SLOTS, half + (h + 1) * Q_SLOTS)
            c = slice(h * LANE, (h + 1) * LANE)
            o = acc_ref[r1, c] / l_ref[r1, :] - lam * (acc_ref[r2, c] / l_ref[r2, :])
            return _rms(o, g) * (1.0 - lam_init)

        _write_heads(o_ref, head, heads)


def _block_diag_queries(q, heads):
    B, tq, _ = q.shape
    q4 = jnp.pad(q.reshape(B, tq, heads, LANE), ((0, 0), (0, Q_SLOTS - tq), (0, 0), (0, 0)))
    qh = jnp.transpose(q4, (0, 2, 1, 3))
    eye = jnp.eye(heads, dtype=q.dtype)
    bd = qh[:, :, :, None, :] * eye[None, :, None, :, None]
    return bd.reshape(B, heads * Q_SLOTS, heads * LANE).astype(BF16)


def _decode_attention(kind, q, k_new, v_new, pool_k, pool_v, page_table, layer, norm_g, extra, out_rows,
                      group):
    B, tq, width = q.shape
    heads = width // LANE
    depth, n_pool, page = pool_k.shape[:3]
    n_pages = page_table.shape[1]
    group = _pick(n_pages, group, 1)
    n_groups = n_pages // group
    pool_k = pool_k.reshape(depth, n_pool, page, width)
    pool_v = pool_v.reshape(depth, n_pool, page, width)
    pad = ((0, 0), (0, page - tq), (0, 0))
    k_new, v_new = jnp.pad(k_new, pad), jnp.pad(v_new, pad)
    rows = heads * Q_SLOTS
    if kind == "sb":
        qbd = _block_diag_queries(q, heads)
    else:
        first = (jnp.arange(width) % LANE) < LANE // 2
        qbd = jnp.concatenate([_block_diag_queries(jnp.where(first, q, 0.0), heads),
                               _block_diag_queries(jnp.where(first, 0.0, q), heads)], axis=1)
    qrows = qbd.shape[1]

    per_b = lambda shape: pl.BlockSpec((None,) + shape, lambda b, p, pt: (b, 0, 0))
    const = lambda shape: pl.BlockSpec(shape, lambda b, p, pt: (0,) * len(shape))

    def page_spec(gi):
        def index(b, p, pt):
            grp = n_groups - jnp.maximum(p, 1)
            return (layer, pt[b, grp * group + gi], 0, 0)
        return pl.BlockSpec((None, None, page, width), index)

    pages = [page_spec(gi) for gi in range(group)]
    vmem = 2 * (2 * group + 2) * _nbytes((page, width), F32) + 4 * _nbytes((qrows, width), F32)
    vmem += 4 * _nbytes((group * page, width), BF16) + 16 * _nbytes((qrows, group * page), F32)
    if kind == "sb":
        body = functools.partial(_sb_decode_body, group=group, heads=heads, scale=LANE ** -0.5)
        tail_in = [norm_g, _cumsum_weights()]
        tail_specs = [const((1, LANE)), const((2 * LANE, 2 * LANE))]
        scratch = [pltpu.VMEM((rows, width), F32), pltpu.VMEM((rows, LANE), F32)]
    else:
        lam_init, lq1, lk1, lq2, lk2 = extra
        body = functools.partial(_df_decode_body, group=group, heads=heads, scale=(LANE // 2) ** -0.5,
                                 lam_init=lam_init)
        tail_in = [norm_g, lq1, lk1, lq2, lk2]
        tail_specs = [const((1, LANE))] + [const((1, LANE // 2))] * 4
        scratch = [pltpu.VMEM((qrows, LANE), F32), pltpu.VMEM((qrows, LANE), F32),
                   pltpu.VMEM((qrows, width), F32)]
    return pl.pallas_call(
        body,
        grid_spec=pltpu.PrefetchScalarGridSpec(
            num_scalar_prefetch=1,
            grid=(B, n_groups + 1),
            in_specs=[per_b((qrows, width)), per_b((page, width)), per_b((page, width))]
            + pages + pages + tail_specs,
            out_specs=per_b((out_rows, width)),
            scratch_shapes=scratch,
        ),
        out_shape=jax.ShapeDtypeStruct((B, out_rows, width), BF16),
        compiler_params=_params(2, vmem),
        name=kind + "_attention_decode",
    )(page_table, qbd, k_new, v_new, *([pool_k] * group), *([pool_v] * group), *tail_in)


def _layer(x, xb, batch, seq, t_valid, layer, depth, w, s_hg_past, cache, page_table, dims):
    h_sb, h_hg, h_df = dims
    w_sb, w_hg = h_sb * LANE, h_hg * LANE
    alpha = (2 * depth) ** 0.25
    lam_init = 0.8 - 0.6 * math.exp(-0.3 * layer)
    proj = _matmul(xb, w["in"], F32, 1024, 512, 4096, "in_projection")
    c_hg, c_df = 3 * h_sb, 3 * h_sb + 4 * h_hg
    df_extra = (lam_init, w["lq1"], w["lk1"], w["lq2"], w["lk2"])
    p3 = proj.reshape(batch, seq, -1)

    def cols(c0, n):
        return p3[:, :t_valid, c0 * LANE:(c0 + n) * LANE]

    sb_k, sb_v = cols(h_sb, h_sb), cols(2 * h_sb, h_sb)
    df_k, df_v = cols(c_df + h_df, h_df), cols(c_df + 2 * h_df, h_df)
    if cache is None:
        o_sb = _prompt_attention("sb", proj, batch, seq, h_sb, 0, w["sb_g"], None, 256)
        o_df = _prompt_attention("df", proj, batch, seq, h_df, c_df, w["df_g"], df_extra, 256)
    else:
        o_sb = _decode_attention("sb", cols(0, h_sb), sb_k, sb_v, cache[0], cache[1], page_table, layer,
                                 w["sb_g"], None, seq, 4).reshape(batch * seq, -1)
        o_df = _decode_attention("df", cols(c_df, h_df), df_k, df_v, cache[2], cache[3], page_table, layer,
                                 w["df_g"], df_extra, seq, 4).reshape(batch * seq, -1)
    o_hg, s_hg = _hgrn(proj, batch, seq, h_hg, c_hg, w["hg_lb"], w["hg_g"], s_hg_past, layer, t_valid)
    o = jnp.concatenate([o_sb, o_hg, o_df], axis=1)
    m = _matmul(o, w["out"], F32, 1024, 512, 4096, "out_projection")
    h, hb = _residual_layernorm(x, m, w["ln1_g"], w["ln1_b"], alpha, "layernorm_mixer")
    hid = _gateup(hb, w["gate"], w["up"], 1024, 256)
    f = _matmul(hid, w["down"], F32, 1024, 512, 5504, "swiglu_down")
    y, yb = _residual_layernorm(h, f, w["ln2_g"], w["ln2_b"], alpha, "layernorm_ffn")
    heads4 = lambda a, n: a.reshape(batch, t_valid, n, LANE)
    return y, yb, (heads4(sb_k, h_sb), heads4(sb_v, h_sb), heads4(df_k, h_df), heads4(df_v, h_df), s_hg)


def kernel(x_prompt, x_sample, cache_sb_k, cache_sb_v, cache_df_k, cache_df_v, state_hg, page_table, w_in, w_out, sb_norm_g, hg_lb, hg_norm_g, df_lq1, df_lk1, df_lq2, df_lk2, df_norm_g, ln1_g, ln1_b, w_gate, w_up, w_down, ln2_g, ln2_b):
    depth = w_in.shape[0]
    bp, seq, d_model = x_prompt.shape
    bs, dec_seq, _ = x_sample.shape
    dims = (cache_sb_k.shape[3], state_hg.shape[2], cache_df_k.shape[3])
    assert dec_seq <= Q_SLOTS and cache_sb_k.shape[4] == LANE and state_hg.shape[3:] == (LANE, LANE)
    dec_rows = HG_CHUNK
    row = lambda a, l: a[l][None, :]

    xp = x_prompt.reshape(bp * seq, d_model)
    xs = jnp.pad(x_sample, ((0, 0), (0, dec_rows - dec_seq), (0, 0))).reshape(bs * dec_rows, d_model)
    xpb, xsb = xp.astype(BF16), xs.astype(BF16)
    cache = (cache_sb_k, cache_sb_v, cache_df_k, cache_df_v)
    s_zero = jnp.zeros((bp,) + state_hg.shape[2:], state_hg.dtype)
    new_p, new_s = [], []
    for layer in range(depth):
        w = {"in": w_in[layer].astype(BF16), "out": w_out[layer].astype(BF16),
             "gate": w_gate[layer].astype(BF16), "up": w_up[layer].astype(BF16),
             "down": w_down[layer].astype(BF16),
             "sb_g": row(sb_norm_g, layer), "hg_g": row(hg_norm_g, layer), "df_g": row(df_norm_g, layer),
             "hg_lb": hg_lb, "lq1": row(df_lq1, layer), "lk1": row(df_lk1, layer),
             "lq2": row(df_lq2, layer), "lk2": row(df_lk2, layer),
             "ln1_g": row(ln1_g, layer), "ln1_b": row(ln1_b, layer),
             "ln2_g": row(ln2_g, layer), "ln2_b": row(ln2_b, layer)}
        xp, xpb, outs_p = _layer(xp, xpb, bp, seq, seq, layer, depth, w, s_zero, None, None, dims)
        xs, xsb, outs_s = _layer(xs, xsb, bs, dec_rows, dec_seq, layer, depth, w, state_hg[layer], cache,
                                 page_table, dims)
        new_p.append(outs_p)
        new_s.append(outs_s)
    y_prompt = xp.reshape(bp, seq, d_model)
    y_sample = xs.reshape(bs, dec_rows, d_model)[:, :dec_seq]
    stack = lambda outs, i: jnp.stack([o[i] for o in outs])
    return (y_prompt, y_sample,
            stack(new_p, 0), stack(new_p, 1), stack(new_p, 2), stack(new_p, 3), stack(new_p, 4),
            stack(new_s, 0), stack(new_s, 1), stack(new_s, 2), stack(new_s, 3), stack(new_s, 4))
```

```python
import functools
import math

import jax
import jax.numpy as jnp
from jax import lax
from jax.experimental import pallas as pl
from jax.experimental.pallas import tpu as pltpu

F32 = jnp.float32
BF16 = jnp.bfloat16

LANE = 128
SUBLANE = 8
VMEM_CAP = 60000 * 1024
HG_CHUNK = 16
Q_SLOTS = SUBLANE
LN_EPS = 1e-5
RMS_EPS = 1e-6
MASK_NEG = -1e30

_NT = (((1,), (1,)), ((), ()))
_TN = (((0,), (0,)), ((), ()))


def _params(n_axes, vmem_bytes):
    limit = int(min(VMEM_CAP, max(32 * 1024 * 1024, vmem_bytes)))
    return pltpu.CompilerParams(dimension_semantics=("arbitrary",) * n_axes, vmem_limit_bytes=limit)


def _pick(dim, pref, align):
    if dim <= pref:
        return dim
    t = (pref // align) * align
    while t >= align:
        if dim % t == 0:
            return t
        t -= align
    return dim


def _heads_per_step(heads, col0, pref):
    hb = pref
    while heads % hb or col0 % hb:
        hb -= 1
    return hb


def _nbytes(shape, dtype):
    return math.prod(shape) * jnp.dtype(dtype).itemsize


def _mm_body(x_ref, w_ref, o_ref, *acc, nk):
    part = jnp.dot(x_ref[...], w_ref[...], preferred_element_type=F32)
    if nk == 1:
        o_ref[...] = part.astype(o_ref.dtype)
        return
    acc_ref, = acc
    k = pl.program_id(2)

    @pl.when(k == 0)
    def _():
        acc_ref[...] = part

    @pl.when(k > 0)
    def _():
        acc_ref[...] += part

    @pl.when(k == nk - 1)
    def _():
        o_ref[...] = acc_ref[...].astype(o_ref.dtype)


def _matmul(x, w, out_dtype, tm, tn, tk, name):
    M, K = x.shape
    N = w.shape[1]
    tm, tn, tk = _pick(M, tm, SUBLANE), _pick(N, tn, LANE), _pick(K, tk, LANE)
    nk = K // tk
    vmem = 2 * (_nbytes((tm, tk), BF16) + _nbytes((tk, tn), BF16) + _nbytes((tm, tn), out_dtype))
    vmem += 3 * _nbytes((tm, tn), F32)
    return pl.pallas_call(
        functools.partial(_mm_body, nk=nk),
        grid=(M // tm, N // tn, nk),
        in_specs=[pl.BlockSpec((tm, tk), lambda i, j, k: (i, k)),
                  pl.BlockSpec((tk, tn), lambda i, j, k: (k, j))],
        out_specs=pl.BlockSpec((tm, tn), lambda i, j, k: (i, j)),
        out_shape=jax.ShapeDtypeStruct((M, N), out_dtype),
        scratch_shapes=[pltpu.VMEM((tm, tn), F32)] if nk > 1 else [],
        compiler_params=_params(3, vmem),
        name=name,
    )(x, w)


def _in_proj_body(x_ref, w_ref, o_ref):
    res = jnp.dot(x_ref[...], w_ref[...], preferred_element_type=F32)
    nb, cols, rows, _ = o_ref.shape
    for b in range(nb):
        for c in range(cols):
            o_ref[b, c] = res[b * rows:(b + 1) * rows, c * LANE:(c + 1) * LANE]


def _in_projection(x, w, batch, seq):
    M, K = x.shape
    N = w.shape[1]
    tn = _pick(N, 512, LANE)
    if seq >= 1024:
        nb, rows = 1, _pick(seq, 1024, SUBLANE)
    else:
        nb, rows = _pick(batch, max(1, 1024 // seq), 1), seq
    tm = nb * rows
    per_seq = seq // rows
    if nb == 1:
        out_index = lambda i, j: (i // per_seq, j, i % per_seq, 0)
    else:
        out_index = lambda i, j: (i, j, 0, 0)
    vmem = 2 * (_nbytes((tm, K), BF16) + _nbytes((K, tn), BF16) + _nbytes((tm, tn), F32))
    vmem += 3 * _nbytes((tm, tn), F32)
    return pl.pallas_call(
        _in_proj_body,
        grid=(M // tm, N // tn),
        in_specs=[pl.BlockSpec((tm, K), lambda i, j: (i, 0)),
                  pl.BlockSpec((K, tn), lambda i, j: (0, j))],
        out_specs=pl.BlockSpec((nb, tn // LANE, rows, LANE), out_index),
        out_shape=jax.ShapeDtypeStruct((batch, N // LANE, seq, LANE), F32),
        compiler_params=_params(2, vmem),
        name="in_projection",
    )(x, w)


def _sigmoid(x):
    return 1.0 / (1.0 + jnp.exp(-x))


def _gateup_body(x_ref, wg_ref, wu_ref, o_ref):
    x = x_ref[...]
    g = jnp.dot(x, wg_ref[...], preferred_element_type=F32)
    u = jnp.dot(x, wu_ref[...], preferred_element_type=F32)
    o_ref[...] = (g * _sigmoid(g) * u).astype(o_ref.dtype)


def _gateup(x, wg, wu, tm, tn):
    M, K = x.shape
    N = wg.shape[1]
    tm, tn = _pick(M, tm, SUBLANE), _pick(N, tn, LANE)
    vmem = 2 * (_nbytes((tm, K), BF16) + 2 * _nbytes((K, tn), BF16) + _nbytes((tm, tn), BF16))
    vmem += 4 * _nbytes((tm, tn), F32)
    return pl.pallas_call(
        _gateup_body,
        grid=(M // tm, N // tn),
        in_specs=[pl.BlockSpec((tm, K), lambda i, j: (i, 0)),
                  pl.BlockSpec((K, tn), lambda i, j: (0, j)),
                  pl.BlockSpec((K, tn), lambda i, j: (0, j))],
        out_specs=pl.BlockSpec((tm, tn), lambda i, j: (i, j)),
        out_shape=jax.ShapeDtypeStruct((M, N), BF16),
        compiler_params=_params(2, vmem),
        name="swiglu_gate_up",
    )(x, wg, wu)


def _ln_body(x_ref, m_ref, g_ref, b_ref, o_ref, ob_ref, *, alpha):
    v = alpha * x_ref[...] + m_ref[...]
    mu = jnp.mean(v, axis=-1, keepdims=True)
    d = v - mu
    var = jnp.mean(d * d, axis=-1, keepdims=True)
    y = d * lax.rsqrt(var + LN_EPS) * g_ref[...] + b_ref[...]
    o_ref[...] = y
    ob_ref[...] = y.astype(BF16)


def _residual_layernorm(x, m, g, b, alpha, name):
    M, D = x.shape
    tm = _pick(M, 256, SUBLANE)
    vmem = 2 * (3 * _nbytes((tm, D), F32) + _nbytes((tm, D), BF16)) + 4 * _nbytes((tm, D), F32)
    row = pl.BlockSpec((tm, D), lambda i: (i, 0))
    vec = pl.BlockSpec((1, D), lambda i: (0, 0))
    return pl.pallas_call(
        functools.partial(_ln_body, alpha=alpha),
        grid=(M // tm,),
        in_specs=[row, row, vec, vec],
        out_specs=[row, row],
        out_shape=[jax.ShapeDtypeStruct((M, D), F32), jax.ShapeDtypeStruct((M, D), BF16)],
        compiler_params=_params(1, vmem),
        name=name,
    )(x, m, g, b)


def _softplus(z):
    return jnp.maximum(z, 0.0) + jnp.log(1.0 + jnp.exp(-jnp.abs(z)))


def _rms(o, g):
    return o * lax.rsqrt(jnp.mean(o * o, axis=-1, keepdims=True) + RMS_EPS) * g


def _lanes(x, n):
    reps = n // LANE
    return x if reps == 1 else jnp.concatenate([x] * reps, axis=1)


def _cumsum_weights():
    j = jnp.arange(LANE)[:, None]
    s = jnp.arange(LANE)[None, :]
    blk = jnp.concatenate([(j > s).astype(BF16), jnp.ones((LANE, LANE), BF16)], axis=1)
    return jnp.concatenate([blk, blk], axis=0)


def _sb_weights(z, valid, run, wcs):
    sp = _softplus(z)
    log_keep = -sp if valid is None else jnp.where(valid, -sp, 0.0)
    log_beta = z - sp
    groups = z.shape[1] // LANE
    ws = [None] * groups
    for g in reversed(range(groups)):
        sl = slice(g * LANE, (g + 1) * LANE)
        lk = log_keep[:, sl]
        hi = lk.astype(BF16)
        lo = (lk - hi.astype(F32)).astype(BF16)
        cs = jnp.dot(jnp.concatenate([hi, lo], axis=1), wcs, preferred_element_type=F32)
        ws[g] = jnp.exp(log_beta[:, sl] + cs[:, :LANE] + run)
        run = run + cs[:, LANE:]
    w = ws[0] if groups == 1 else jnp.concatenate(ws, axis=1)
    if valid is not None:
        w = jnp.where(valid, w, 0.0)
    return w, run


def _softmax_update(s, m_prev, l_prev):
    m_new = jnp.maximum(m_prev, jnp.max(s, axis=1, keepdims=True))
    alpha = jnp.exp(m_prev - m_new)
    p = jnp.exp(s - _lanes(m_new, s.shape[1]))
    return p, alpha, m_new, alpha * l_prev + jnp.sum(p, axis=1, keepdims=True)


def _df_lambda(lq1_ref, lk1_ref, lq2_ref, lk2_ref, lam_init):
    s1 = jnp.sum(lq1_ref[...] * lk1_ref[...], axis=1, keepdims=True)
    s2 = jnp.sum(lq2_ref[...] * lk2_ref[...], axis=1, keepdims=True)
    return jnp.exp(s1) - jnp.exp(s2) + lam_init


def _split_halves(q):
    first = lax.broadcasted_iota(jnp.int32, q.shape, 1) < LANE // 2
    return jnp.concatenate([jnp.where(first, q, 0.0), jnp.where(first, 0.0, q)], axis=0)


def _cast_kv(i, k_ref, v_ref, kb_ref, vb_ref):
    @pl.when(i == 0)
    def _():
        kb_ref[...] = k_ref[...].astype(BF16)
        vb_ref[...] = v_ref[...].astype(BF16)


def _sb_prompt_body(q_ref, k_ref, v_ref, g_ref, wcs_ref, o_ref, kb_ref, vb_ref, acc_ref, run_ref,
                    *, hb, tq, scale):
    i = pl.program_id(2)
    _cast_kv(i, k_ref, v_ref, kb_ref, vb_ref)
    qb = [q_ref[h].astype(BF16) for h in range(hb)]
    wcs = wcs_ref[...]
    acc_ref[...] = jnp.zeros_like(acc_ref)
    run_ref[...] = jnp.zeros_like(run_ref)

    def tile(j, valid):
        ks = pl.multiple_of(j * tq, tq)
        for h in range(hb):
            kt = kb_ref[h, pl.ds(ks, tq), :]
            vt = vb_ref[h, pl.ds(ks, tq), :]
            z = lax.dot_general(qb[h], kt, _NT, preferred_element_type=F32) * scale
            w, run = _sb_weights(z, valid, run_ref[h], wcs)
            run_ref[h] = run
            acc_ref[h] += jnp.dot(w.astype(BF16), vt, preferred_element_type=F32)

    row = lax.broadcasted_iota(jnp.int32, (tq, tq), 0)
    col = lax.broadcasted_iota(jnp.int32, (tq, tq), 1)
    tile(i, col < row)

    def body(jj, c):
        tile(i - 1 - jj, None)
        return c

    lax.fori_loop(0, i, body, 0)
    g = g_ref[...]
    for h in range(hb):
        o_ref[:, h * LANE:(h + 1) * LANE] = _rms(acc_ref[h], g).astype(o_ref.dtype)


def _df_prompt_body(q_ref, k_ref, v_ref, g_ref, lq1_ref, lk1_ref, lq2_ref, lk2_ref, o_ref,
                    kb_ref, vb_ref, m_ref, l_ref, acc_ref, *, hb, tq, scale, lam_init):
    i = pl.program_id(2)
    _cast_kv(i, k_ref, v_ref, kb_ref, vb_ref)
    qq = [_split_halves(q_ref[h]).astype(BF16) for h in range(hb)]
    m_ref[...] = jnp.full_like(m_ref, MASK_NEG)
    l_ref[...] = jnp.zeros_like(l_ref)
    acc_ref[...] = jnp.zeros_like(acc_ref)

    def tile(j, valid):
        ks = pl.multiple_of(j * tq, tq)
        for h in range(hb):
            kt = kb_ref[h, pl.ds(ks, tq), :]
            vt = vb_ref[h, pl.ds(ks, tq), :]
            s = lax.dot_general(qq[h], kt, _NT, preferred_element_type=F32) * scale
            if valid is not None:
                s = jnp.where(valid, s, MASK_NEG)
            p, alpha, m_ref[h], l_ref[h] = _softmax_update(s, m_ref[h], l_ref[h])
            acc_ref[h] = acc_ref[h] * alpha + jnp.dot(p.astype(BF16), vt, preferred_element_type=F32)

    row = lax.broadcasted_iota(jnp.int32, (2 * tq, tq), 0)
    col = lax.broadcasted_iota(jnp.int32, (2 * tq, tq), 1)
    row = jnp.where(row >= tq, row - tq, row)
    tile(i, col <= row)

    def body(jj, c):
        tile(jj, None)
        return c

    lax.fori_loop(0, i, body, 0)
    g = g_ref[...]
    lam = _df_lambda(lq1_ref, lk1_ref, lq2_ref, lk2_ref, lam_init)
    for h in range(hb):
        o = acc_ref[h] / l_ref[h]
        o = o[:tq] - lam * o[tq:]
        o_ref[:, h * LANE:(h + 1) * LANE] = (_rms(o, g) * (1.0 - lam_init)).astype(o_ref.dtype)


def _prompt_attention(kind, proj, heads, col0, norm_g, extra, tq, hb):
    batch, _, seq, _ = proj.shape
    tq = _pick(seq, tq, LANE)
    nq = seq // tq
    hb = _heads_per_step(heads, col0, hb)
    qspec = pl.BlockSpec((None, hb, tq, LANE), lambda b, h, i: (b, col0 // hb + h, i, 0))
    kspec = pl.BlockSpec((None, hb, seq, LANE), lambda b, h, i: (b, (col0 + heads) // hb + h, 0, 0))
    vspec = pl.BlockSpec((None, hb, seq, LANE), lambda b, h, i: (b, (col0 + 2 * heads) // hb + h, 0, 0))
    const = lambda shape: pl.BlockSpec(shape, lambda b, h, i: (0,) * len(shape))
    kv_scratch = [pltpu.VMEM((hb, seq, LANE), BF16), pltpu.VMEM((hb, seq, LANE), BF16)]
    vmem = hb * (4 * _nbytes((seq, LANE), F32) + 4 * _nbytes((tq, LANE), F32) + 2 * _nbytes((seq, LANE), BF16))
    vmem += hb * 16 * _nbytes((2 * tq, tq), F32)
    if kind == "sb":
        body = functools.partial(_sb_prompt_body, hb=hb, tq=tq, scale=LANE ** -0.5)
        ins = [proj, proj, proj, norm_g, _cumsum_weights()]
        specs = [qspec, kspec, vspec, const((1, LANE)), const((2 * LANE, 2 * LANE))]
        scratch = kv_scratch + [pltpu.VMEM((hb, tq, LANE), F32)] * 2
    else:
        lam_init, lq1, lk1, lq2, lk2 = extra
        body = functools.partial(_df_prompt_body, hb=hb, tq=tq, scale=(LANE // 2) ** -0.5, lam_init=lam_init)
        ins = [proj, proj, proj, norm_g, lq1, lk1, lq2, lk2]
        specs = [qspec, kspec, vspec, const((1, LANE))] + [const((1, LANE // 2))] * 4
        scratch = kv_scratch + [pltpu.VMEM((hb, 2 * tq, LANE), F32)] * 3
    return pl.pallas_call(
        body,
        grid=(batch, heads // hb, nq),
        in_specs=specs,
        out_specs=pl.BlockSpec((tq, hb * LANE), lambda b, h, i: (b * nq + i, h)),
        out_shape=jax.ShapeDtypeStruct((batch * seq, heads * LANE), BF16),
        scratch_shapes=scratch,
        compiler_params=_params(3, vmem),
        name=kind + "_attention_prompt",
    )(*ins)


def _hgrn_body(q_ref, f_ref, i_ref, gate_ref, lbp_ref, gn_ref, lmat_ref, s0_ref, o_ref, s_ref,
               st_ref, qa_ref, key_ref, bl_ref, oraw_ref, *, hb, rows, layer, depth, t_valid, nt):
    t = pl.program_id(2)

    @pl.when(t == 0)
    def _():
        for h in range(hb):
            st_ref[h] = s0_ref[h].T

    lmat = lmat_ref[...]
    for h in range(hb):
        lbp = lbp_ref[:, h * LANE:(h + 1) * LANE]
        prow = [lbp[r:r + 1] for r in range(depth)]
        mx = functools.reduce(jnp.maximum, prow)
        es = [jnp.exp(r - mx) for r in prow]
        lb = sum(es[1:layer + 1], jnp.zeros_like(mx)) / sum(es[1:], es[0])
        p = f_ref[h]
        log_f = -_softplus(-p) + jnp.log1p(lb * jnp.exp(-p))
        key = (1.0 - lb) * (1.0 / (1.0 + jnp.exp(p)))
        if t_valid is not None:
            ok = (t * rows + lax.broadcasted_iota(jnp.int32, p.shape, 0)) < t_valid
            log_f = jnp.where(ok, log_f, 0.0)
            key = jnp.where(ok, key, 0.0)
        hi = log_f.astype(BF16)
        r1 = log_f - hi.astype(F32)
        mid = r1.astype(BF16)
        lo = (r1 - mid.astype(F32)).astype(BF16)
        bl_ref[h] = jnp.dot(lmat, jnp.concatenate([hi, mid, lo], axis=0), preferred_element_type=F32)
        qv = q_ref[h]
        qa_ref[h] = qv * _sigmoid(qv)
        key_ref[h] = key

    rowi = lax.broadcasted_iota(jnp.int32, (HG_CHUNK, LANE), 0)

    def chunk(c, carry):
        r0 = pl.multiple_of(c * HG_CHUNK, HG_CHUNK)
        for h in range(hb):
            blc = bl_ref[h, pl.ds(r0, HG_CHUNK), :]
            qac = qa_ref[h, pl.ds(r0, HG_CHUNK), :]
            kc = key_ref[h, pl.ds(r0, HG_CHUNK), :]
            vc = i_ref[h, pl.ds(r0, HG_CHUNK), :]
            blast = blc[HG_CHUNK - 1:HG_CHUNK]
            st = st_ref[h]
            o = lax.dot_general((qac * jnp.exp(blc)).astype(BF16), st.astype(BF16), _NT,
                                preferred_element_type=F32)
            for s in range(HG_CHUNK):
                e = jnp.exp(jnp.where(rowi >= s, blc - blc[s:s + 1], MASK_NEG))
                cv = jnp.sum(qac * e * kc[s:s + 1], axis=1, keepdims=True)
                o = o + cv * vc[s:s + 1]
            oraw_ref[h, pl.ds(r0, HG_CHUNK), :] = o
            kt = kc * jnp.exp(blast - blc)
            upd = lax.dot_general(vc.astype(BF16), kt.astype(BF16), _TN, preferred_element_type=F32)
            st_ref[h] = st * jnp.exp(blast) + upd
        return carry

    lax.fori_loop(0, rows // HG_CHUNK, chunk, 0)
    gn = gn_ref[...]
    for h in range(hb):
        gate = gate_ref[h]
        o_ref[:, h * LANE:(h + 1) * LANE] = (_rms(oraw_ref[h], gn) * (gate * _sigmoid(gate))).astype(o_ref.dtype)

    @pl.when(t == nt - 1)
    def _():
        for h in range(hb):
            s_ref[h] = st_ref[h].T


def _hgrn(proj, heads, col0, hg_lb, norm_g, s0, layer, t_valid, hb):
    batch, _, seq, _ = proj.shape
    rows = _pick(seq, 256, HG_CHUNK)
    nt = seq // rows
    depth = hg_lb.shape[0]
    hb = _heads_per_step(heads, col0, hb)
    r = jnp.arange(rows)
    lmat = ((r[:, None] // HG_CHUNK == r[None, :] // HG_CHUNK) & (r[None, :] <= r[:, None])).astype(BF16)
    lmat = jnp.concatenate([lmat] * 3, axis=1)

    def col(k):
        return pl.BlockSpec((None, hb, rows, LANE), lambda b, h, t: (b, (col0 + k * heads) // hb + h, t, 0))

    state = pl.BlockSpec((None, hb, LANE, LANE), lambda b, h, t: (b, h, 0, 0))
    vmem = hb * (2 * (5 * _nbytes((rows, LANE), F32) + 4 * _nbytes((LANE, LANE), F32)) + 16 * _nbytes((rows, LANE), F32))
    vmem += 2 * _nbytes((rows, 3 * rows), BF16)
    return pl.pallas_call(
        functools.partial(_hgrn_body, hb=hb, rows=rows, layer=layer, depth=depth,
                          t_valid=None if t_valid == seq else t_valid, nt=nt),
        grid=(batch, heads // hb, nt),
        in_specs=[col(0), col(1), col(2), col(3),
                  pl.BlockSpec((depth, hb * LANE), lambda b, h, t: (0, h)),
                  pl.BlockSpec((1, LANE), lambda b, h, t: (0, 0)),
                  pl.BlockSpec((rows, 3 * rows), lambda b, h, t: (0, 0)),
                  state],
        out_specs=[pl.BlockSpec((rows, hb * LANE), lambda b, h, t: (b * nt + t, h)), state],
        out_shape=[jax.ShapeDtypeStruct((batch * seq, heads * LANE), BF16),
                   jax.ShapeDtypeStruct(s0.shape, F32)],
        scratch_shapes=[pltpu.VMEM((hb, LANE, LANE), F32)] + [pltpu.VMEM((hb, rows, LANE), F32)] * 4,
        compiler_params=_params(3, vmem),
        name="hgrn2",
    )(proj, proj, proj, proj, hg_lb, norm_g, lmat, s0)


def _head_scores(qb, k_refs, heads, rows_per_head):
    out = []
    for h in range(heads):
        kh = [r[h].astype(BF16) for r in k_refs]
        kh = kh[0] if len(kh) == 1 else jnp.concatenate(kh, axis=0)
        out.append(lax.dot_general(qb[h * rows_per_head:(h + 1) * rows_per_head], kh, _NT,
                                   preferred_element_type=F32))
    return jnp.concatenate(out, axis=0)


def _head_values(w, v_refs, heads, rows_per_head):
    out = []
    for h in range(heads):
        vh = [r[h].astype(BF16) for r in v_refs]
        vh = vh[0] if len(vh) == 1 else jnp.concatenate(vh, axis=0)
        out.append(jnp.dot(w[h * rows_per_head:(h + 1) * rows_per_head].astype(BF16), vh,
                           preferred_element_type=F32))
    return jnp.concatenate(out, axis=0)


def _write_heads(o_ref, head_fn, heads):
    pad = jnp.zeros((o_ref.shape[0] - Q_SLOTS, LANE), F32)
    for h in range(heads):
        o_ref[:, h * LANE:(h + 1) * LANE] = jnp.concatenate([head_fn(h), pad], axis=0).astype(o_ref.dtype)


def _sb_decode_body(pt_ref, q_ref, kn_ref, vn_ref, *rest, group, heads, scale):
    k_refs, v_refs = rest[:group], rest[group:2 * group]
    g_ref, wcs_ref, o_ref, acc_ref, run_ref = rest[2 * group:]
    p = pl.program_id(1)
    qb = q_ref[...]
    wcs = wcs_ref[...]

    @pl.when(p == 0)
    def _():
        z = _head_scores(qb, [kn_ref], heads, Q_SLOTS) * scale
        tok = lax.broadcasted_iota(jnp.int32, z.shape, 0) & (Q_SLOTS - 1)
        col = lax.broadcasted_iota(jnp.int32, z.shape, 1)
        w, run = _sb_weights(z, col < tok, jnp.zeros(run_ref.shape, F32), wcs)
        run_ref[...] = run
        acc_ref[...] = _head_values(w, [vn_ref], heads, Q_SLOTS)

    @pl.when(p > 0)
    def _():
        z = _head_scores(qb, k_refs, heads, Q_SLOTS) * scale
        w, run = _sb_weights(z, None, run_ref[...], wcs)
        run_ref[...] = run
        acc_ref[...] += _head_values(w, v_refs, heads, Q_SLOTS)

    @pl.when(p == pl.num_programs(1) - 1)
    def _():
        g = g_ref[...]
        _write_heads(o_ref, lambda h: _rms(acc_ref[h * Q_SLOTS:(h + 1) * Q_SLOTS, :], g), heads)


def _df_decode_body(pt_ref, q_ref, kn_ref, vn_ref, *rest, group, heads, scale, lam_init):
    k_refs, v_refs = rest[:group], rest[group:2 * group]
    g_ref, lq1_ref, lk1_ref, lq2_ref, lk2_ref, o_ref, m_ref, l_ref, acc_ref = rest[2 * group:]
    p = pl.program_id(1)
    qb = q_ref[...]
    rph = 2 * Q_SLOTS

    def step(s, v_parts, first):
        m_prev = jnp.full(m_ref.shape, MASK_NEG, F32) if first else m_ref[...]
        l_prev = jnp.zeros(l_ref.shape, F32) if first else l_ref[...]
        pr, alpha, m_ref[...], l_ref[...] = _softmax_update(s, m_prev, l_prev)
        pv = _head_values(pr, v_parts, heads, rph)
        acc_ref[...] = pv if first else acc_ref[...] * alpha + pv

    @pl.when(p == 0)
    def _():
        s = _head_scores(qb, [kn_ref], heads, rph) * scale
        tok = lax.broadcasted_iota(jnp.int32, s.shape, 0) & (Q_SLOTS - 1)
        col = lax.broadcasted_iota(jnp.int32, s.shape, 1)
        step(jnp.where(col <= tok, s, MASK_NEG), [vn_ref], True)

    @pl.when(p > 0)
    def _():
        step(_head_scores(qb, k_refs, heads, rph) * scale, v_refs, False)

    @pl.when(p == pl.num_programs(1) - 1)
    def _():
        g = g_ref[...]
        lam = _df_lambda(lq1_ref, lk1_ref, lq2_ref, lk2_ref, lam_init)

        def head(h):
            r1 = slice(h * rph, h * rph + Q_SLOTS)
            r2 = slice(h * rph + Q_SLOTS, (h + 1) * rph)
            o = acc_ref[r1, :] / l_ref[r1, :] - lam * (acc_ref[r2, :] / l_ref[r2, :])
            return _rms(o, g) * (1.0 - lam_init)

        _write_heads(o_ref, head, heads)


def _decode_attention(kind, q, k_new, v_new, pool_k, pool_v, page_table, layer, norm_g, extra, out_rows,
                      group):
    B, heads, tq, _ = q.shape
    page = pool_k.shape[3]
    n_pages = page_table.shape[1]
    group = _pick(n_pages, group, 1)
    n_groups = n_pages // group
    pad = ((0, 0), (0, 0), (0, page - tq), (0, 0))
    k_new, v_new = jnp.pad(k_new, pad), jnp.pad(v_new, pad)
    q = jnp.pad(q, ((0, 0), (0, 0), (0, Q_SLOTS - tq), (0, 0)))
    if kind == "sb":
        qrows = heads * Q_SLOTS
        qb = q.reshape(B, qrows, LANE).astype(BF16)
    else:
        qrows = heads * 2 * Q_SLOTS
        first = jnp.arange(LANE) < LANE // 2
        qb = jnp.stack([jnp.where(first, q, 0.0), jnp.where(first, 0.0, q)], axis=2)
        qb = qb.reshape(B, qrows, LANE).astype(BF16)

    per_b = lambda shape: pl.BlockSpec((None,) + shape, lambda b, p, pt: (b,) + (0,) * len(shape))
    const = lambda shape: pl.BlockSpec(shape, lambda b, p, pt: (0,) * len(shape))

    def page_spec(gi):
        def index(b, p, pt):
            grp = n_groups - jnp.maximum(p, 1)
            return (layer, pt[b, grp * group + gi], 0, 0, 0)
        return pl.BlockSpec((None, None, heads, page, LANE), index)

    pages = [page_spec(gi) for gi in range(group)]
    vmem = 2 * (2 * group + 2) * _nbytes((heads, page, LANE), F32)
    vmem += 4 * _nbytes((heads, group * page, LANE), BF16) + 24 * _nbytes((qrows, group * page), F32)
    if kind == "sb":
        body = functools.partial(_sb_decode_body, group=group, heads=heads, scale=LANE ** -0.5)
        tail_in = [norm_g, _cumsum_weights()]
        tail_specs = [const((1, LANE)), const((2 * LANE, 2 * LANE))]
        scratch = [pltpu.VMEM((qrows, LANE), F32), pltpu.VMEM((qrows, LANE), F32)]
    else:
        lam_init, lq1, lk1, lq2, lk2 = extra
        body = functools.partial(_df_decode_body, group=group, heads=heads, scale=(LANE // 2) ** -0.5,
                                 lam_init=lam_init)
        tail_in = [norm_g, lq1, lk1, lq2, lk2]
        tail_specs = [const((1, LANE))] + [const((1, LANE // 2))] * 4
        scratch = [pltpu.VMEM((qrows, LANE), F32)] * 3
    return pl.pallas_call(
        body,
        grid_spec=pltpu.PrefetchScalarGridSpec(
            num_scalar_prefetch=1,
            grid=(B, n_groups + 1),
            in_specs=[per_b((qrows, LANE)), per_b((heads, page, LANE)), per_b((heads, page, LANE))]
            + pages + pages + tail_specs,
            out_specs=per_b((out_rows, heads * LANE)),
            scratch_shapes=scratch,
        ),
        out_shape=jax.ShapeDtypeStruct((B, out_rows, heads * LANE), BF16),
        compiler_params=_params(2, vmem),
        name=kind + "_attention_decode",
    )(page_table, qb, k_new, v_new, *([pool_k] * group), *([pool_v] * group), *tail_in)


def _layer(x, xb, batch, seq, t_valid, layer, depth, w, s_hg_past, cache, page_table, dims):
    h_sb, h_hg, h_df = dims
    alpha = (2 * depth) ** 0.25
    lam_init = 0.8 - 0.6 * math.exp(-0.3 * layer)
    proj = _in_projection(xb, w["in"], batch, seq)
    c_hg, c_df = 3 * h_sb, 3 * h_sb + 4 * h_hg
    df_extra = (lam_init, w["lq1"], w["lk1"], w["lq2"], w["lk2"])

    def cols(c0, n):
        return proj[:, c0:c0 + n, :t_valid]

    sb_k, sb_v = cols(h_sb, h_sb), cols(2 * h_sb, h_sb)
    df_k, df_v = cols(c_df + h_df, h_df), cols(c_df + 2 * h_df, h_df)
    if cache is None:
        o_sb = _prompt_attention("sb", proj, h_sb, 0, w["sb_g"], None, 256, 2)
        o_df = _prompt_attention("df", proj, h_df, c_df, w["df_g"], df_extra, 256, 2)
    else:
        o_sb = _decode_attention("sb", cols(0, h_sb), sb_k, sb_v, cache[0], cache[1], page_table, layer,
                                 w["sb_g"], None, seq, 4).reshape(batch * seq, -1)
        o_df = _decode_attention("df", cols(c_df, h_df), df_k, df_v, cache[2], cache[3], page_table, layer,
                                 w["df_g"], df_extra, seq, 4).reshape(batch * seq, -1)
    o_hg, s_hg = _hgrn(proj, h_hg, c_hg, w["hg_lb"], w["hg_g"], s_hg_past, layer, t_valid, 4)
    o = jnp.concatenate([o_sb, o_hg, o_df], axis=1)
    m = _matmul(o, w["out"], F32, 1024, 512, 4096, "out_projection")
    h, hb = _residual_layernorm(x, m, w["ln1_g"], w["ln1_b"], alpha, "layernorm_mixer")
    hid = _gateup(hb, w["gate"], w["up"], 1024, 256)
    f = _matmul(hid, w["down"], F32, 1024, 512, 5504, "swiglu_down")
    y, yb = _residual_layernorm(h, f, w["ln2_g"], w["ln2_b"], alpha, "layernorm_ffn")
    tokens_major = lambda a: jnp.transpose(a, (0, 2, 1, 3))
    return y, yb, (tokens_major(sb_k), tokens_major(sb_v), tokens_major(df_k), tokens_major(df_v), s_hg)


def kernel(x_prompt, x_sample, cache_sb_k, cache_sb_v, cache_df_k, cache_df_v, state_hg, page_table, w_in, w_out, sb_norm_g, hg_lb, hg_norm_g, df_lq1, df_lk1, df_lq2, df_lk2, df_norm_g, ln1_g, ln1_b, w_gate, w_up, w_down, ln2_g, ln2_b):
    depth = w_in.shape[0]
    bp, seq, d_model = x_prompt.shape
    bs, dec_seq, _ = x_sample.shape
    dims = (cache_sb_k.shape[3], state_hg.shape[2], cache_df_k.shape[3])
    assert dec_seq <= Q_SLOTS and cache_sb_k.shape[4] == LANE and state_hg.shape[3:] == (LANE, LANE)
    dec_rows = HG_CHUNK
    row = lambda a, l: a[l][None, :]

    xp = x_prompt.reshape(bp * seq, d_model)
    xs = jnp.pad(x_sample, ((0, 0), (0, dec_rows - dec_seq), (0, 0))).reshape(bs * dec_rows, d_model)
    xpb, xsb = xp.astype(BF16), xs.astype(BF16)
    cache = tuple(jnp.transpose(c, (0, 1, 3, 2, 4)) for c in (cache_sb_k, cache_sb_v, cache_df_k, cache_df_v))
    s_zero = jnp.zeros((bp,) + state_hg.shape[2:], state_hg.dtype)
    new_p, new_s = [], []
    for layer in range(depth):
        w = {"in": w_in[layer].astype(BF16), "out": w_out[layer].astype(BF16),
             "gate": w_gate[layer].astype(BF16), "up": w_up[layer].astype(BF16),
             "down": w_down[layer].astype(BF16),
             "sb_g": row(sb_norm_g, layer), "hg_g": row(hg_norm_g, layer), "df_g": row(df_norm_g, layer),
             "hg_lb": hg_lb, "lq1": row(df_lq1, layer), "lk1": row(df_lk1, layer),
             "lq2": row(df_lq2, layer), "lk2": row(df_lk2, layer),
             "ln1_g": row(ln1_g, layer), "ln1_b": row(ln1_b, layer),
             "ln2_g": row(ln2_g, layer), "ln2_b": row(ln2_b, layer)}
        xp, xpb, outs_p = _layer(xp, xpb, bp, seq, seq, layer, depth, w, s_zero, None, None, dims)
        xs, xsb, outs_s = _layer(xs, xsb, bs, dec_rows, dec_seq, layer, depth, w, state_hg[layer], cache,
                                 page_table, dims)
        new_p.append(outs_p)
        new_s.append(outs_s)
    y_prompt = xp.reshape(bp, seq, d_model)
    y_sample = xs.reshape(bs, dec_rows, d_model)[:, :dec_seq]
    stack = lambda outs, i: jnp.stack([o[i] for o in outs])
    return (y_prompt, y_sample,
            stack(new_p, 0), stack(new_p, 1), stack(new_p, 2), stack(new_p, 3), stack(new_p, 4),
            stack(new_s, 0), stack(new_s, 1), stack(new_s, 2), stack(new_s, 3), stack(new_s, 4))
```

```python
import functools
import math

import jax
import jax.numpy as jnp
from jax import lax
from jax.experimental import pallas as pl
from jax.experimental.pallas import tpu as pltpu

F32 = jnp.float32
BF16 = jnp.bfloat16

LANE = 128
SUBLANE = 8
VMEM_CAP = 60000 * 1024
HG_CHUNK = 16
Q_SLOTS = SUBLANE
LN_EPS = 1e-5
RMS_EPS = 1e-6
MASK_NEG = -1e30
EXP_UNDERFLOW = -104.0

_NT = (((1,), (1,)), ((), ()))
_TN = (((0,), (0,)), ((), ()))


def _params(n_axes, vmem_bytes):
    limit = int(min(VMEM_CAP, max(32 * 1024 * 1024, vmem_bytes)))
    return pltpu.CompilerParams(dimension_semantics=("arbitrary",) * n_axes, vmem_limit_bytes=limit)


def _pick(dim, pref, align):
    if dim <= pref:
        return dim
    t = (pref // align) * align
    while t >= align:
        if dim % t == 0:
            return t
        t -= align
    return dim


def _heads_per_step(heads, col0, pref):
    hb = pref
    while heads % hb or col0 % hb:
        hb -= 1
    return hb


def _nbytes(shape, dtype):
    return math.prod(shape) * jnp.dtype(dtype).itemsize


def _resident(shape, index):
    return pl.BlockSpec(shape, index, pipeline_mode=pl.Buffered(1))


def _mm_body(x_ref, w_ref, o_ref, *acc, nk):
    part = jnp.dot(x_ref[...], w_ref[...].astype(BF16), preferred_element_type=F32)
    if nk == 1:
        o_ref[...] = part.astype(o_ref.dtype)
        return
    acc_ref, = acc
    k = pl.program_id(2)

    @pl.when(k == 0)
    def _():
        acc_ref[...] = part

    @pl.when(k > 0)
    def _():
        acc_ref[...] += part

    @pl.when(k == nk - 1)
    def _():
        o_ref[...] = acc_ref[...].astype(o_ref.dtype)


def _matmul(x, w, layer, tm, tn, tk, name):
    M, K = x.shape
    N = w.shape[2]
    tm, tn, tk = _pick(M, tm, SUBLANE), _pick(N, tn, LANE), _pick(K, tk, LANE)
    nk = K // tk
    vmem = 2 * (_nbytes((tm, tk), BF16) + _nbytes((tk, tn), F32) + _nbytes((tm, tn), F32))
    vmem += 3 * _nbytes((tm, tn), F32) + _nbytes((tk, tn), BF16)
    return pl.pallas_call(
        functools.partial(_mm_body, nk=nk),
        grid=(M // tm, N // tn, nk),
        in_specs=[pl.BlockSpec((tm, tk), lambda i, j, k: (i, k)),
                  pl.BlockSpec((None, tk, tn), lambda i, j, k: (layer, k, j))],
        out_specs=pl.BlockSpec((tm, tn), lambda i, j, k: (i, j)),
        out_shape=jax.ShapeDtypeStruct((M, N), F32),
        scratch_shapes=[pltpu.VMEM((tm, tn), F32)] if nk > 1 else [],
        compiler_params=_params(3, vmem),
        name=name,
    )(x, w)


def _out_proj_body(*refs, widths):
    x_refs = refs[:len(widths)]
    w_ref, o_ref = refs[len(widths):]
    acc, r0 = None, 0
    for x_ref, kw in zip(x_refs, widths):
        part = jnp.dot(x_ref[...], w_ref[r0:r0 + kw, :].astype(BF16), preferred_element_type=F32)
        acc = part if acc is None else acc + part
        r0 += kw
    o_ref[...] = acc


def _out_projection(xs, w, layer, tm, tn):
    M = xs[0].shape[0]
    widths = tuple(x.shape[1] for x in xs)
    K, N = w.shape[1:]
    assert sum(widths) == K
    tm, tn = _pick(M, tm, SUBLANE), _pick(N, tn, LANE)
    vmem = _nbytes((tm, K), BF16) + 2 * (_nbytes((K, tn), F32) + _nbytes((tm, tn), F32))
    vmem += 3 * _nbytes((tm, tn), F32) + _nbytes((K, tn), BF16)
    return pl.pallas_call(
        functools.partial(_out_proj_body, widths=widths),
        grid=(M // tm, N // tn),
        in_specs=[_resident((tm, kw), lambda i, j: (i, 0)) for kw in widths]
        + [pl.BlockSpec((None, K, tn), lambda i, j: (layer, 0, j))],
        out_specs=pl.BlockSpec((tm, tn), lambda i, j: (i, j)),
        out_shape=jax.ShapeDtypeStruct((M, N), F32),
        compiler_params=_params(2, vmem),
        name="out_projection",
    )(*xs, w)


def _in_proj_body(x_ref, w_ref, *refs, kv_tiles):
    o_ref, *kv_refs = refs[len(refs) - 1 - len(kv_tiles):]
    res = jnp.dot(x_ref[...], w_ref[...].astype(BF16), preferred_element_type=F32)
    nb, cols, rows, _ = o_ref.shape

    for b in range(nb):
        for c in range(cols):
            o_ref[b, c] = res[b * rows:(b + 1) * rows, c * LANE:(c + 1) * LANE]
    j = pl.program_id(1)
    for (j0, n), ref in zip(kv_tiles, kv_refs):

        @pl.when((j >= j0) & (j < j0 + n))
        def _(ref=ref):
            ref[...] = o_ref[...]


def _in_projection(x, w, layer, batch, seq, kv_cols, kv_prev):
    M, K = x.shape
    depth, _, N = w.shape
    cols = N // LANE
    kc = 2
    while cols % kc or any(c0 % kc or n % kc for c0, n in kv_cols):
        kc //= 2
    tn = kc * LANE
    if seq >= 1024:
        nb, rows = 1, _pick(seq, 2048, SUBLANE)
    else:
        nb, rows = _pick(batch, max(1, 1024 // seq), 1), seq
    tm = nb * rows
    per_seq = seq // rows
    where = (lambda i: (i // per_seq, i % per_seq)) if nb == 1 else (lambda i: (i, 0))
    kv_tiles = tuple((c0 // kc, n // kc) for c0, n in kv_cols)

    def kv_spec(j0, n):
        def index(i, j):
            b, r = where(i)
            return (layer, b, jnp.clip(j - j0, 0, n - 1), r, 0)
        return pl.BlockSpec((None, nb, kc, rows, LANE), index)

    def proj_index(i, j):
        b, r = where(i)
        return (b, j, r, 0)

    n_prev = len(kv_prev)
    vmem = _nbytes((tm, K), BF16) + 2 * _nbytes((K, tn), F32) + _nbytes((K, tn), BF16)
    vmem += (2 * (1 + len(kv_cols)) + 5) * _nbytes((tm, tn), F32)
    outs = pl.pallas_call(
        functools.partial(_in_proj_body, kv_tiles=kv_tiles),
        grid=(M // tm, cols // kc),
        in_specs=[_resident((tm, K), lambda i, j: (i, 0)),
                  pl.BlockSpec((None, K, tn), lambda i, j: (layer, 0, j))]
        + [pl.BlockSpec(memory_space=pl.ANY)] * n_prev,
        out_specs=[pl.BlockSpec((nb, kc, rows, LANE), proj_index)] + [kv_spec(j0, n) for j0, n in kv_tiles],
        out_shape=[jax.ShapeDtypeStruct((batch, cols, seq, LANE), F32)]
        + [jax.ShapeDtypeStruct((depth, batch, n, seq, LANE), F32) for _, n in kv_cols],
        input_output_aliases={2 + a: 1 + a for a in range(n_prev)},
        compiler_params=_params(2, vmem),
        name="in_projection",
    )(x, w, *kv_prev)
    return outs[0], tuple(outs[1:])


def _sigmoid(x):
    return 1.0 / (1.0 + jnp.exp(-x))


def _gateup_body(x_ref, wg_ref, wu_ref, o_ref):
    x = x_ref[...]
    g = jnp.dot(x, wg_ref[...].astype(BF16), preferred_element_type=F32)
    u = jnp.dot(x, wu_ref[...].astype(BF16), preferred_element_type=F32)
    o_ref[...] = (g * _sigmoid(g) * u).astype(o_ref.dtype)


def _gateup(x, wg, wu, layer, tm, tn):
    M, K = x.shape
    N = wg.shape[2]
    tm, tn = _pick(M, tm, SUBLANE), _pick(N, tn, LANE)
    vmem = _nbytes((tm, K), BF16) + 2 * (2 * _nbytes((K, tn), F32) + _nbytes((tm, tn), BF16))
    vmem += 4 * _nbytes((tm, tn), F32) + 2 * _nbytes((K, tn), BF16)
    wspec = pl.BlockSpec((None, K, tn), lambda i, j: (layer, 0, j))
    return pl.pallas_call(
        _gateup_body,
        grid=(M // tm, N // tn),
        in_specs=[_resident((tm, K), lambda i, j: (i, 0)), wspec, wspec],
        out_specs=pl.BlockSpec((tm, tn), lambda i, j: (i, j)),
        out_shape=jax.ShapeDtypeStruct((M, N), BF16),
        compiler_params=_params(2, vmem),
        name="swiglu_gate_up",
    )(x, wg, wu)


def _ln_body(x_ref, m_ref, g_ref, b_ref, o_ref, ob_ref, *, alpha):
    v = alpha * x_ref[...] + m_ref[...]
    mu = jnp.mean(v, axis=-1, keepdims=True)
    d = v - mu
    var = jnp.mean(d * d, axis=-1, keepdims=True)
    y = d * lax.rsqrt(var + LN_EPS) * g_ref[...] + b_ref[...]
    o_ref[...] = y
    ob_ref[...] = y.astype(BF16)


def _residual_layernorm(x, m, g, b, alpha, name):
    M, D = x.shape
    tm = _pick(M, 256, SUBLANE)
    vmem = 2 * (3 * _nbytes((tm, D), F32) + _nbytes((tm, D), BF16)) + 4 * _nbytes((tm, D), F32)
    row = pl.BlockSpec((tm, D), lambda i: (i, 0))
    vec = pl.BlockSpec((1, D), lambda i: (0, 0))
    return pl.pallas_call(
        functools.partial(_ln_body, alpha=alpha),
        grid=(M // tm,),
        in_specs=[row, row, vec, vec],
        out_specs=[row, row],
        out_shape=[jax.ShapeDtypeStruct((M, D), F32), jax.ShapeDtypeStruct((M, D), BF16)],
        compiler_params=_params(1, vmem),
        name=name,
    )(x, m, g, b)


def _softplus(z):
    return jnp.maximum(z, 0.0) + jnp.log(1.0 + jnp.exp(-jnp.abs(z)))


def _rms(o, g):
    return o * lax.rsqrt(jnp.mean(o * o, axis=-1, keepdims=True) + RMS_EPS) * g


def _lanes(x, n):
    reps = n // LANE
    return x if reps == 1 else jnp.concatenate([x] * reps, axis=1)


def _cumsum_weights():
    j = jnp.arange(LANE)[:, None]
    s = jnp.arange(LANE)[None, :]
    blk = jnp.concatenate([(j > s).astype(BF16), jnp.ones((LANE, LANE), BF16)], axis=1)
    return jnp.concatenate([blk, blk], axis=0)


def _sb_weights(z, valid, run, wcs):
    sp = _softplus(z)
    log_keep = -sp if valid is None else jnp.where(valid, -sp, 0.0)
    log_beta = z - sp
    groups = z.shape[1] // LANE
    ws = [None] * groups
    for g in reversed(range(groups)):
        sl = slice(g * LANE, (g + 1) * LANE)
        lk = log_keep[:, sl]
        hi = lk.astype(BF16)
        lo = (lk - hi.astype(F32)).astype(BF16)
        cs = jnp.dot(jnp.concatenate([hi, lo], axis=1), wcs, preferred_element_type=F32)
        ws[g] = jnp.exp(log_beta[:, sl] + cs[:, :LANE] + run)
        run = run + cs[:, LANE:]
    w = ws[0] if groups == 1 else jnp.concatenate(ws, axis=1)
    if valid is not None:
        w = jnp.where(valid, w, 0.0)
    return w, run


def _softmax_update(s, m_prev, l_prev):
    m_new = jnp.maximum(m_prev, jnp.max(s, axis=1, keepdims=True))
    alpha = jnp.exp(m_prev - m_new)
    p = jnp.exp(s - _lanes(m_new, s.shape[1]))
    return p, alpha, m_new, alpha * l_prev + jnp.sum(p, axis=1, keepdims=True)


def _df_lambda(lq1_ref, lk1_ref, lq2_ref, lk2_ref, lam_init):
    s1 = jnp.sum(lq1_ref[...] * lk1_ref[...], axis=1, keepdims=True)
    s2 = jnp.sum(lq2_ref[...] * lk2_ref[...], axis=1, keepdims=True)
    return jnp.exp(s1) - jnp.exp(s2) + lam_init


def _split_halves(q):
    first = lax.broadcasted_iota(jnp.int32, q.shape, 1) < LANE // 2
    return jnp.concatenate([jnp.where(first, q, 0.0), jnp.where(first, 0.0, q)], axis=0)


def _cast_kv(i, k_ref, v_ref, kb_ref, vb_ref):
    @pl.when(i == 0)
    def _():
        kb_ref[...] = k_ref[...].astype(BF16)
        vb_ref[...] = v_ref[...].astype(BF16)


def _sb_prompt_body(q_ref, k_ref, v_ref, g_ref, wcs_ref, o_ref, kb_ref, vb_ref, acc_ref, run_ref,
                    *, hb, tq, scale):
    i = pl.program_id(2)
    _cast_kv(i, k_ref, v_ref, kb_ref, vb_ref)
    qb = [q_ref[h].astype(BF16) for h in range(hb)]
    wcs = wcs_ref[...]
    acc_ref[...] = jnp.zeros_like(acc_ref)
    run_ref[...] = jnp.zeros_like(run_ref)

    def tile(j, valid):
        ks = pl.multiple_of(j * tq, tq)
        live = None
        for h in range(hb):
            kt = kb_ref[h, pl.ds(ks, tq), :]
            vt = vb_ref[h, pl.ds(ks, tq), :]
            z = lax.dot_general(qb[h], kt, _NT, preferred_element_type=F32) * scale
            w, run = _sb_weights(z, valid, run_ref[h], wcs)
            run_ref[h] = run
            acc_ref[h] += jnp.dot(w.astype(BF16), vt, preferred_element_type=F32)
            top = jnp.max(run)
            live = top if live is None else jnp.maximum(live, top)
        return live

    row = lax.broadcasted_iota(jnp.int32, (tq, tq), 0)
    col = lax.broadcasted_iota(jnp.int32, (tq, tq), 1)
    live = tile(i, col < row)

    def more(c):
        return (c[0] < i) & (c[1] > EXP_UNDERFLOW)

    def body(c):
        return c[0] + 1, tile(i - 1 - c[0], None)

    lax.while_loop(more, body, (jnp.int32(0), live))
    g = g_ref[...]
    for h in range(hb):
        o_ref[:, h * LANE:(h + 1) * LANE] = _rms(acc_ref[h], g).astype(o_ref.dtype)


def _df_prompt_body(q_ref, k_ref, v_ref, g_ref, lq1_ref, lk1_ref, lq2_ref, lk2_ref, o_ref,
                    kb_ref, vb_ref, m_ref, l_ref, acc_ref, *, hb, tq, scale, lam_init):
    i = pl.program_id(2)
    _cast_kv(i, k_ref, v_ref, kb_ref, vb_ref)
    qq = [_split_halves(q_ref[h]).astype(BF16) for h in range(hb)]
    m_ref[...] = jnp.full_like(m_ref, MASK_NEG)
    l_ref[...] = jnp.zeros_like(l_ref)
    acc_ref[...] = jnp.zeros_like(acc_ref)

    def tile(j, valid):
        ks = pl.multiple_of(j * tq, tq)
        for h in range(hb):
            kt = kb_ref[h, pl.ds(ks, tq), :]
            vt = vb_ref[h, pl.ds(ks, tq), :]
            s = lax.dot_general(qq[h], kt, _NT, preferred_element_type=F32) * scale
            if valid is not None:
                s = jnp.where(valid, s, MASK_NEG)
            p, alpha, m_ref[h], l_ref[h] = _softmax_update(s, m_ref[h], l_ref[h])
            acc_ref[h] = acc_ref[h] * alpha + jnp.dot(p.astype(BF16), vt, preferred_element_type=F32)

    row = lax.broadcasted_iota(jnp.int32, (2 * tq, tq), 0)
    col = lax.broadcasted_iota(jnp.int32, (2 * tq, tq), 1)
    row = jnp.where(row >= tq, row - tq, row)
    tile(i, col <= row)

    def body(jj, c):
        tile(jj, None)
        return c

    lax.fori_loop(0, i, body, 0)
    g = g_ref[...]
    lam = _df_lambda(lq1_ref, lk1_ref, lq2_ref, lk2_ref, lam_init)
    for h in range(hb):
        o = acc_ref[h] / l_ref[h]
        o = o[:tq] - lam * o[tq:]
        o_ref[:, h * LANE:(h + 1) * LANE] = (_rms(o, g) * (1.0 - lam_init)).astype(o_ref.dtype)


def _prompt_attention(kind, proj, kv_k, kv_v, layer, col0, norm_g, extra, tq, hb):
    batch, _, seq, _ = proj.shape
    heads = kv_k.shape[2]
    tq = _pick(seq, tq, LANE)
    nq = seq // tq
    hb = _heads_per_step(heads, col0, hb)
    qspec = pl.BlockSpec((None, hb, tq, LANE), lambda b, h, i: (b, col0 // hb + h, i, 0))
    kvspec = pl.BlockSpec((None, None, hb, seq, LANE), lambda b, h, i: (layer, b, h, 0, 0))
    const = lambda shape: pl.BlockSpec(shape, lambda b, h, i: (0,) * len(shape))
    kv_scratch = [pltpu.VMEM((hb, seq, LANE), BF16), pltpu.VMEM((hb, seq, LANE), BF16)]
    vmem = hb * (4 * _nbytes((seq, LANE), F32) + 4 * _nbytes((tq, LANE), F32) + 2 * _nbytes((seq, LANE), BF16))
    vmem += hb * 16 * _nbytes((2 * tq, tq), F32)
    if kind == "sb":
        body = functools.partial(_sb_prompt_body, hb=hb, tq=tq, scale=LANE ** -0.5)
        ins = [proj, kv_k, kv_v, norm_g, _cumsum_weights()]
        specs = [qspec, kvspec, kvspec, const((1, LANE)), const((2 * LANE, 2 * LANE))]
        scratch = kv_scratch + [pltpu.VMEM((hb, tq, LANE), F32)] * 2
    else:
        lam_init, lq1, lk1, lq2, lk2 = extra
        body = functools.partial(_df_prompt_body, hb=hb, tq=tq, scale=(LANE // 2) ** -0.5, lam_init=lam_init)
        ins = [proj, kv_k, kv_v, norm_g, lq1, lk1, lq2, lk2]
        specs = [qspec, kvspec, kvspec, const((1, LANE))] + [const((1, LANE // 2))] * 4
        scratch = kv_scratch + [pltpu.VMEM((hb, 2 * tq, LANE), F32)] * 3
    return pl.pallas_call(
        body,
        grid=(batch, heads // hb, nq),
        in_specs=specs,
        out_specs=pl.BlockSpec((tq, hb * LANE), lambda b, h, i: (b * nq + i, h)),
        out_shape=jax.ShapeDtypeStruct((batch * seq, heads * LANE), BF16),
        scratch_shapes=scratch,
        compiler_params=_params(3, vmem),
        name=kind + "_attention_prompt",
    )(*ins)


def _hgrn_body(q_ref, f_ref, i_ref, gate_ref, lbp_ref, gn_ref, lmat_ref, s0_ref, o_ref, s_ref,
               st_ref, qa_ref, key_ref, bl_ref, oraw_ref, *, hb, rows, layer, depth, t_valid, nt):
    t = pl.program_id(2)

    @pl.when(t == 0)
    def _():
        for h in range(hb):
            st_ref[h] = s0_ref[h].T

    lmat = lmat_ref[...]
    for h in range(hb):
        lbp = lbp_ref[:, h * LANE:(h + 1) * LANE]
        prow = [lbp[r:r + 1] for r in range(depth)]
        mx = functools.reduce(jnp.maximum, prow)
        es = [jnp.exp(r - mx) for r in prow]
        lb = sum(es[1:layer + 1], jnp.zeros_like(mx)) / sum(es[1:], es[0])
        p = f_ref[h]
        log_f = -_softplus(-p) + jnp.log1p(lb * jnp.exp(-p))
        key = (1.0 - lb) * (1.0 / (1.0 + jnp.exp(p)))
        if t_valid is not None:
            ok = (t * rows + lax.broadcasted_iota(jnp.int32, p.shape, 0)) < t_valid
            log_f = jnp.where(ok, log_f, 0.0)
            key = jnp.where(ok, key, 0.0)
        hi = log_f.astype(BF16)
        r1 = log_f - hi.astype(F32)
        mid = r1.astype(BF16)
        lo = (r1 - mid.astype(F32)).astype(BF16)
        bl_ref[h] = jnp.dot(lmat, jnp.concatenate([hi, mid, lo], axis=0), preferred_element_type=F32)
        qv = q_ref[h]
        qa_ref[h] = qv * _sigmoid(qv)
        key_ref[h] = key

    rowi = lax.broadcasted_iota(jnp.int32, (HG_CHUNK, LANE), 0)

    def chunk(c, carry):
        r0 = pl.multiple_of(c * HG_CHUNK, HG_CHUNK)
        for h in range(hb):
            blc = bl_ref[h, pl.ds(r0, HG_CHUNK), :]
            qac = qa_ref[h, pl.ds(r0, HG_CHUNK), :]
            kc = key_ref[h, pl.ds(r0, HG_CHUNK), :]
            vc = i_ref[h, pl.ds(r0, HG_CHUNK), :]
            blast = blc[HG_CHUNK - 1:HG_CHUNK]
            st = st_ref[h]
            o = lax.dot_general((qac * jnp.exp(blc)).astype(BF16), st.astype(BF16), _NT,
                                preferred_element_type=F32)
            for s in range(HG_CHUNK):
                e = jnp.exp(jnp.where(rowi >= s, blc - blc[s:s + 1], MASK_NEG))
                cv = jnp.sum(qac * e * kc[s:s + 1], axis=1, keepdims=True)
                o = o + cv * vc[s:s + 1]
            oraw_ref[h, pl.ds(r0, HG_CHUNK), :] = o
            kt = kc * jnp.exp(blast - blc)
            upd = lax.dot_general(vc.astype(BF16), kt.astype(BF16), _TN, preferred_element_type=F32)
            st_ref[h] = st * jnp.exp(blast) + upd
        return carry

    lax.fori_loop(0, rows // HG_CHUNK, chunk, 0)
    gn = gn_ref[...]
    for h in range(hb):
        gate = gate_ref[h]
        o_ref[:, h * LANE:(h + 1) * LANE] = (_rms(oraw_ref[h], gn) * (gate * _sigmoid(gate))).astype(o_ref.dtype)

    @pl.when(t == nt - 1)
    def _():
        for h in range(hb):
            s_ref[h] = st_ref[h].T


def _hgrn(proj, heads, col0, hg_lb, norm_g, s0, layer, t_valid, hb):
    batch, _, seq, _ = proj.shape
    rows = _pick(seq, 256, HG_CHUNK)
    nt = seq // rows
    depth = hg_lb.shape[0]
    hb = _heads_per_step(heads, col0, hb)
    r = jnp.arange(rows)
    lmat = ((r[:, None] // HG_CHUNK == r[None, :] // HG_CHUNK) & (r[None, :] <= r[:, None])).astype(BF16)
    lmat = jnp.concatenate([lmat] * 3, axis=1)

    def col(k):
        return pl.BlockSpec((None, hb, rows, LANE), lambda b, h, t: (b, (col0 + k * heads) // hb + h, t, 0))

    state = pl.BlockSpec((None, hb, LANE, LANE), lambda b, h, t: (b, h, 0, 0))
    vmem = hb * (2 * (5 * _nbytes((rows, LANE), F32) + 4 * _nbytes((LANE, LANE), F32)) + 16 * _nbytes((rows, LANE), F32))
    vmem += 2 * _nbytes((rows, 3 * rows), BF16)
    return pl.pallas_call(
        functools.partial(_hgrn_body, hb=hb, rows=rows, layer=layer, depth=depth,
                          t_valid=None if t_valid == seq else t_valid, nt=nt),
        grid=(batch, heads // hb, nt),
        in_specs=[col(0), col(1), col(2), col(3),
                  pl.BlockSpec((depth, hb * LANE), lambda b, h, t: (0, h)),
                  pl.BlockSpec((1, LANE), lambda b, h, t: (0, 0)),
                  pl.BlockSpec((rows, 3 * rows), lambda b, h, t: (0, 0)),
                  state],
        out_specs=[pl.BlockSpec((rows, hb * LANE), lambda b, h, t: (b * nt + t, h)), state],
        out_shape=[jax.ShapeDtypeStruct((batch * seq, heads * LANE), BF16),
                   jax.ShapeDtypeStruct(s0.shape, F32)],
        scratch_shapes=[pltpu.VMEM((hb, LANE, LANE), F32)] + [pltpu.VMEM((hb, rows, LANE), F32)] * 4,
        compiler_params=_params(3, vmem),
        name="hgrn2",
    )(proj, proj, proj, proj, hg_lb, norm_g, lmat, s0)


def _head_scores(qb, k_refs, heads, rows_per_head):
    out = []
    for h in range(heads):
        kh = [r[h].astype(BF16) for r in k_refs]
        kh = kh[0] if len(kh) == 1 else jnp.concatenate(kh, axis=0)
        out.append(lax.dot_general(qb[h * rows_per_head:(h + 1) * rows_per_head], kh, _NT,
                                   preferred_element_type=F32))
    return jnp.concatenate(out, axis=0)


def _head_values(w, v_refs, heads, rows_per_head):
    out = []
    for h in range(heads):
        vh = [r[h].astype(BF16) for r in v_refs]
        vh = vh[0] if len(vh) == 1 else jnp.concatenate(vh, axis=0)
        out.append(jnp.dot(w[h * rows_per_head:(h + 1) * rows_per_head].astype(BF16), vh,
                           preferred_element_type=F32))
    return jnp.concatenate(out, axis=0)


def _write_heads(o_ref, head_fn, heads):
    pad = jnp.zeros((o_ref.shape[0] - Q_SLOTS, LANE), F32)
    for h in range(heads):
        o_ref[:, h * LANE:(h + 1) * LANE] = jnp.concatenate([head_fn(h), pad], axis=0).astype(o_ref.dtype)


def _sb_decode_body(pt_ref, q_ref, kn_ref, vn_ref, *rest, group, heads, scale):
    k_refs, v_refs = rest[:group], rest[group:2 * group]
    g_ref, wcs_ref, o_ref, acc_ref, run_ref = rest[2 * group:]
    p = pl.program_id(1)
    qb = q_ref[...]
    wcs = wcs_ref[...]

    @pl.when(p == 0)
    def _():
        z = _head_scores(qb, [kn_ref], heads, Q_SLOTS) * scale
        tok = lax.broadcasted_iota(jnp.int32, z.shape, 0) & (Q_SLOTS - 1)
        col = lax.broadcasted_iota(jnp.int32, z.shape, 1)
        w, run = _sb_weights(z, col < tok, jnp.zeros(run_ref.shape, F32), wcs)
        run_ref[...] = run
        acc_ref[...] = _head_values(w, [vn_ref], heads, Q_SLOTS)

    @pl.when(p > 0)
    def _():
        z = _head_scores(qb, k_refs, heads, Q_SLOTS) * scale
        w, run = _sb_weights(z, None, run_ref[...], wcs)
        run_ref[...] = run
        acc_ref[...] += _head_values(w, v_refs, heads, Q_SLOTS)

    @pl.when(p == pl.num_programs(1) - 1)
    def _():
        g = g_ref[...]
        _write_heads(o_ref, lambda h: _rms(acc_ref[h * Q_SLOTS:(h + 1) * Q_SLOTS, :], g), heads)


def _df_decode_body(pt_ref, q_ref, kn_ref, vn_ref, *rest, group, heads, scale, lam_init):
    k_refs, v_refs = rest[:group], rest[group:2 * group]
    g_ref, lq1_ref, lk1_ref, lq2_ref, lk2_ref, o_ref, m_ref, l_ref, acc_ref = rest[2 * group:]
    p = pl.program_id(1)
    qb = q_ref[...]
    rph = 2 * Q_SLOTS

    def step(s, v_parts, first):
        m_prev = jnp.full(m_ref.shape, MASK_NEG, F32) if first else m_ref[...]
        l_prev = jnp.zeros(l_ref.shape, F32) if first else l_ref[...]
        pr, alpha, m_ref[...], l_ref[...] = _softmax_update(s, m_prev, l_prev)
        pv = _head_values(pr, v_parts, heads, rph)
        acc_ref[...] = pv if first else acc_ref[...] * alpha + pv

    @pl.when(p == 0)
    def _():
        s = _head_scores(qb, [kn_ref], heads, rph) * scale
        tok = lax.broadcasted_iota(jnp.int32, s.shape, 0) & (Q_SLOTS - 1)
        col = lax.broadcasted_iota(jnp.int32, s.shape, 1)
        step(jnp.where(col <= tok, s, MASK_NEG), [vn_ref], True)

    @pl.when(p > 0)
    def _():
        step(_head_scores(qb, k_refs, heads, rph) * scale, v_refs, False)

    @pl.when(p == pl.num_programs(1) - 1)
    def _():
        g = g_ref[...]
        lam = _df_lambda(lq1_ref, lk1_ref, lq2_ref, lk2_ref, lam_init)

        def head(h):
            r1 = slice(h * rph, h * rph + Q_SLOTS)
            r2 = slice(h * rph + Q_SLOTS, (h + 1) * rph)
            o = acc_ref[r1, :] / l_ref[r1, :] - lam * (acc_ref[r2, :] / l_ref[r2, :])
            return _rms(o, g) * (1.0 - lam_init)

        _write_heads(o_ref, head, heads)


def _decode_attention(kind, q, k_new, v_new, pool_k, pool_v, page_table, layer, norm_g, extra, out_rows,
                      group):
    B, heads, tq, _ = q.shape
    page = pool_k.shape[3]
    n_pages = page_table.shape[1]
    group = _pick(n_pages, group, 1)
    n_groups = n_pages // group
    pad = ((0, 0), (0, 0), (0, page - tq), (0, 0))
    k_new, v_new = jnp.pad(k_new, pad), jnp.pad(v_new, pad)
    q = jnp.pad(q, ((0, 0), (0, 0), (0, Q_SLOTS - tq), (0, 0)))
    if kind == "sb":
        qrows = heads * Q_SLOTS
        qb = q.reshape(B, qrows, LANE).astype(BF16)
    else:
        qrows = heads * 2 * Q_SLOTS
        first = jnp.arange(LANE) < LANE // 2
        qb = jnp.stack([jnp.where(first, q, 0.0), jnp.where(first, 0.0, q)], axis=2)
        qb = qb.reshape(B, qrows, LANE).astype(BF16)

    per_b = lambda shape: pl.BlockSpec((None,) + shape, lambda b, p, pt: (b,) + (0,) * len(shape))
    const = lambda shape: pl.BlockSpec(shape, lambda b, p, pt: (0,) * len(shape))

    def page_spec(gi):
        def index(b, p, pt):
            grp = n_groups - jnp.maximum(p, 1)
            return (layer, pt[b, grp * group + gi], 0, 0, 0)
        return pl.BlockSpec((None, None, heads, page, LANE), index)

    pages = [page_spec(gi) for gi in range(group)]
    vmem = 2 * (2 * group + 2) * _nbytes((heads, page, LANE), F32)
    vmem += 4 * _nbytes((heads, group * page, LANE), BF16) + 24 * _nbytes((qrows, group * page), F32)
    if kind == "sb":
        body = functools.partial(_sb_decode_body, group=group, heads=heads, scale=LANE ** -0.5)
        tail_in = [norm_g, _cumsum_weights()]
        tail_specs = [const((1, LANE)), const((2 * LANE, 2 * LANE))]
        scratch = [pltpu.VMEM((qrows, LANE), F32), pltpu.VMEM((qrows, LANE), F32)]
    else:
        lam_init, lq1, lk1, lq2, lk2 = extra
        body = functools.partial(_df_decode_body, group=group, heads=heads, scale=(LANE // 2) ** -0.5,
                                 lam_init=lam_init)
        tail_in = [norm_g, lq1, lk1, lq2, lk2]
        tail_specs = [const((1, LANE))] + [const((1, LANE // 2))] * 4
        scratch = [pltpu.VMEM((qrows, LANE), F32)] * 3
    return pl.pallas_call(
        body,
        grid_spec=pltpu.PrefetchScalarGridSpec(
            num_scalar_prefetch=1,
            grid=(B, n_groups + 1),
            in_specs=[per_b((qrows, LANE)), per_b((heads, page, LANE)), per_b((heads, page, LANE))]
            + pages + pages + tail_specs,
            out_specs=per_b((out_rows, heads * LANE)),
            scratch_shapes=scratch,
        ),
        out_shape=jax.ShapeDtypeStruct((B, out_rows, heads * LANE), BF16),
        compiler_params=_params(2, vmem),
        name=kind + "_attention_decode",
    )(page_table, qb, k_new, v_new, *([pool_k] * group), *([pool_v] * group), *tail_in)


def _layer(x, xb, batch, seq, t_valid, layer, w, kv_prev, s_hg_past, cache, page_table, dims):
    h_sb, h_hg, h_df = dims
    depth = w["in"].shape[0]
    alpha = (2 * depth) ** 0.25
    lam_init = 0.8 - 0.6 * math.exp(-0.3 * layer)
    c_hg, c_df = 3 * h_sb, 3 * h_sb + 4 * h_hg
    kv_cols = ((h_sb, h_sb), (2 * h_sb, h_sb), (c_df + h_df, h_df), (c_df + 2 * h_df, h_df))
    proj, kv = _in_projection(xb, w["in"], layer, batch, seq, kv_cols, kv_prev)
    row = lambda name: w[name][layer][None, :]
    df_extra = (lam_init, row("lq1"), row("lk1"), row("lq2"), row("lk2"))
    if cache is None:
        o_sb = _prompt_attention("sb", proj, kv[0], kv[1], layer, 0, row("sb_g"), None, 256, 2)
        o_df = _prompt_attention("df", proj, kv[2], kv[3], layer, c_df, row("df_g"), df_extra, 256, 2)
    else:
        new = lambda a: a[layer, :, :, :t_valid]
        o_sb = _decode_attention("sb", proj[:, :h_sb, :t_valid], new(kv[0]), new(kv[1]), cache[0], cache[1],
                                 page_table, layer, row("sb_g"), None, seq, 4).reshape(batch * seq, -1)
        o_df = _decode_attention("df", proj[:, c_df:c_df + h_df, :t_valid], new(kv[2]), new(kv[3]), cache[2],
                                 cache[3], page_table, layer, row("df_g"), df_extra, seq, 4
                                 ).reshape(batch * seq, -1)
    o_hg, s_hg = _hgrn(proj, h_hg, c_hg, w["hg_lb"], row("hg_g"), s_hg_past, layer, t_valid, 4)
    m = _out_projection([o_sb, o_hg, o_df], w["out"], layer, 2048, 512)
    h, hb = _residual_layernorm(x, m, row("ln1_g"), row("ln1_b"), alpha, "layernorm_mixer")
    hid = _gateup(hb, w["gate"], w["up"], layer, 2048, 256)
    f = _matmul(hid, w["down"], layer, 1024, 256, 5504, "swiglu_down")
    y, yb = _residual_layernorm(h, f, row("ln2_g"), row("ln2_b"), alpha, "layernorm_ffn")
    return y, yb, kv, s_hg


def kernel(x_prompt, x_sample, cache_sb_k, cache_sb_v, cache_df_k, cache_df_v, state_hg, page_table, w_in, w_out, sb_norm_g, hg_lb, hg_norm_g, df_lq1, df_lk1, df_lq2, df_lk2, df_norm_g, ln1_g, ln1_b, w_gate, w_up, w_down, ln2_g, ln2_b):
    depth = w_in.shape[0]
    bp, seq, d_model = x_prompt.shape
    bs, dec_seq, _ = x_sample.shape
    dims = (cache_sb_k.shape[3], state_hg.shape[2], cache_df_k.shape[3])
    assert dec_seq <= Q_SLOTS and cache_sb_k.shape[4] == LANE and state_hg.shape[3:] == (LANE, LANE)
    dec_rows = HG_CHUNK
    w = {"in": w_in, "out": w_out, "gate": w_gate, "up": w_up, "down": w_down,
         "sb_g": sb_norm_g, "hg_g": hg_norm_g, "df_g": df_norm_g, "hg_lb": hg_lb,
         "lq1": df_lq1, "lk1": df_lk1, "lq2": df_lq2, "lk2": df_lk2,
         "ln1_g": ln1_g, "ln1_b": ln1_b, "ln2_g": ln2_g, "ln2_b": ln2_b}

    xp = x_prompt.reshape(bp * seq, d_model)
    xs = jnp.pad(x_sample, ((0, 0), (0, dec_rows - dec_seq), (0, 0))).reshape(bs * dec_rows, d_model)
    xpb, xsb = xp.astype(BF16), xs.astype(BF16)
    cache = tuple(jnp.transpose(c, (0, 1, 3, 2, 4)) for c in (cache_sb_k, cache_sb_v, cache_df_k, cache_df_v))
    s_zero = jnp.zeros((bp,) + state_hg.shape[2:], state_hg.dtype)
    kv_p, kv_s, st_p, st_s = (), (), [], []
    for layer in range(depth):
        xp, xpb, kv_p, s = _layer(xp, xpb, bp, seq, seq, layer, w, kv_p, s_zero, None, None, dims)
        st_p.append(s)
        xs, xsb, kv_s, s = _layer(xs, xsb, bs, dec_rows, dec_seq, layer, w, kv_s, state_hg[layer], cache,
                                  page_table, dims)
        st_s.append(s)
    y_prompt = xp.reshape(bp, seq, d_model)
    y_sample = xs.reshape(bs, dec_rows, d_model)[:, :dec_seq]
    tokens_major = lambda a, t: jnp.transpose(a[:, :, :, :t], (0, 1, 3, 2, 4))
    return (y_prompt, y_sample,
            *(tokens_major(a, seq) for a in kv_p), jnp.stack(st_p),
            *(tokens_major(a, dec_seq) for a in kv_s), jnp.stack(st_s))
```

```python
import functools
import math

import jax
import jax.numpy as jnp
from jax import lax
from jax.experimental import pallas as pl
from jax.experimental.pallas import tpu as pltpu

F32 = jnp.float32
BF16 = jnp.bfloat16

LANE = 128
SUBLANE = 8
VMEM_CAP = 60000 * 1024
HG_CHUNK = 16
Q_SLOTS = SUBLANE
LN_EPS = 1e-5
RMS_EPS = 1e-6
MASK_NEG = -1e30
EXP_UNDERFLOW = -104.0

_NT = (((1,), (1,)), ((), ()))
_TN = (((0,), (0,)), ((), ()))


def _params(n_axes, vmem_bytes):
    limit = int(min(VMEM_CAP, max(32 * 1024 * 1024, vmem_bytes)))
    return pltpu.CompilerParams(dimension_semantics=("arbitrary",) * n_axes, vmem_limit_bytes=limit)


def _pick(dim, pref, align):
    if dim <= pref:
        return dim
    t = (pref // align) * align
    while t >= align:
        if dim % t == 0:
            return t
        t -= align
    return dim


def _heads_per_step(heads, col0, pref):
    hb = pref
    while heads % hb or col0 % hb:
        hb -= 1
    return hb


def _nbytes(shape, dtype):
    return math.prod(shape) * jnp.dtype(dtype).itemsize


def _resident(shape, index):
    return pl.BlockSpec(shape, index, pipeline_mode=pl.Buffered(1))


def _mm_body(x_ref, w_ref, o_ref):
    o_ref[...] = jnp.dot(x_ref[...], w_ref[...].astype(BF16), preferred_element_type=F32)


def _matmul(x, w, layer, tm, tn, name):
    M, K = x.shape
    N = w.shape[2]
    tm, tn = _pick(M, tm, SUBLANE), _pick(N, tn, LANE)
    vmem = _nbytes((tm, K), BF16) + 2 * (_nbytes((K, tn), w.dtype) + _nbytes((tm, tn), F32))
    vmem += 3 * _nbytes((tm, tn), F32) + _nbytes((K, tn), BF16)
    return pl.pallas_call(
        _mm_body,
        grid=(M // tm, N // tn),
        in_specs=[_resident((tm, K), lambda i, j: (i, 0)),
                  pl.BlockSpec((None, K, tn), lambda i, j: (layer, 0, j))],
        out_specs=pl.BlockSpec((tm, tn), lambda i, j: (i, j)),
        out_shape=jax.ShapeDtypeStruct((M, N), F32),
        compiler_params=_params(2, vmem),
        name=name,
    )(x, w)


def _out_proj_body(*refs, widths):
    x_refs = refs[:len(widths)]
    w_ref, o_ref = refs[len(widths):]
    acc, r0 = None, 0
    for x_ref, kw in zip(x_refs, widths):
        part = jnp.dot(x_ref[...], w_ref[r0:r0 + kw, :].astype(BF16), preferred_element_type=F32)
        acc = part if acc is None else acc + part
        r0 += kw
    o_ref[...] = acc


def _out_projection(xs, w, layer, tm, tn):
    M = xs[0].shape[0]
    widths = tuple(x.shape[1] for x in xs)
    K, N = w.shape[1:]
    assert sum(widths) == K
    tm, tn = _pick(M, tm, SUBLANE), _pick(N, tn, LANE)
    vmem = _nbytes((tm, K), BF16) + 2 * (_nbytes((K, tn), F32) + _nbytes((tm, tn), F32))
    vmem += 3 * _nbytes((tm, tn), F32) + _nbytes((K, tn), BF16)
    return pl.pallas_call(
        functools.partial(_out_proj_body, widths=widths),
        grid=(M // tm, N // tn),
        in_specs=[_resident((tm, kw), lambda i, j: (i, 0)) for kw in widths]
        + [pl.BlockSpec((None, K, tn), lambda i, j: (layer, 0, j))],
        out_specs=pl.BlockSpec((tm, tn), lambda i, j: (i, j)),
        out_shape=jax.ShapeDtypeStruct((M, N), F32),
        compiler_params=_params(2, vmem),
        name="out_projection",
    )(*xs, w)


def _in_proj_body(x_ref, w_ref, *refs, kv_tiles):
    o_ref, *kv_refs = refs[len(refs) - 1 - len(kv_tiles):]
    res = jnp.dot(x_ref[...], w_ref[...].astype(BF16), preferred_element_type=F32)
    nb, cols, rows, _ = o_ref.shape

    for b in range(nb):
        for c in range(cols):
            o_ref[b, c] = res[b * rows:(b + 1) * rows, c * LANE:(c + 1) * LANE]
    j = pl.program_id(1)
    for (j0, n), ref in zip(kv_tiles, kv_refs):

        @pl.when((j >= j0) & (j < j0 + n))
        def _(ref=ref):
            ref[...] = o_ref[...]


def _in_projection(x, w, layer, batch, seq, kv_cols, kv_prev):
    M, K = x.shape
    depth, _, N = w.shape
    cols = N // LANE
    kc = 2
    while cols % kc or any(c0 % kc or n % kc for c0, n in kv_cols):
        kc //= 2
    tn = kc * LANE
    if seq >= 1024:
        nb, rows = 1, _pick(seq, 2048, SUBLANE)
    else:
        nb, rows = _pick(batch, max(1, 1024 // seq), 1), seq
    tm = nb * rows
    per_seq = seq // rows
    where = (lambda i: (i // per_seq, i % per_seq)) if nb == 1 else (lambda i: (i, 0))
    kv_tiles = tuple((c0 // kc, n // kc) for c0, n in kv_cols)

    def kv_spec(j0, n):
        def index(i, j):
            b, r = where(i)
            return (layer, b, jnp.clip(j - j0, 0, n - 1), r, 0)
        return pl.BlockSpec((None, nb, kc, rows, LANE), index)

    def proj_index(i, j):
        b, r = where(i)
        return (b, j, r, 0)

    n_prev = len(kv_prev)
    vmem = _nbytes((tm, K), BF16) + 2 * _nbytes((K, tn), F32) + _nbytes((K, tn), BF16)
    vmem += (2 * (1 + len(kv_cols)) + 5) * _nbytes((tm, tn), F32)
    outs = pl.pallas_call(
        functools.partial(_in_proj_body, kv_tiles=kv_tiles),
        grid=(M // tm, cols // kc),
        in_specs=[_resident((tm, K), lambda i, j: (i, 0)),
                  pl.BlockSpec((None, K, tn), lambda i, j: (layer, 0, j))]
        + [pl.BlockSpec(memory_space=pl.ANY)] * n_prev,
        out_specs=[pl.BlockSpec((nb, kc, rows, LANE), proj_index)] + [kv_spec(j0, n) for j0, n in kv_tiles],
        out_shape=[jax.ShapeDtypeStruct((batch, cols, seq, LANE), F32)]
        + [jax.ShapeDtypeStruct((depth, batch, n, seq, LANE), F32) for _, n in kv_cols],
        input_output_aliases={2 + a: 1 + a for a in range(n_prev)},
        compiler_params=_params(2, vmem),
        name="in_projection",
    )(x, w, *kv_prev)
    return outs[0], tuple(outs[1:])


def _sigmoid(x):
    return 1.0 / (1.0 + jnp.exp(-x))


def _gateup_body(x_ref, wg_ref, wu_ref, o_ref):
    x = x_ref[...]
    g = jnp.dot(x, wg_ref[...].astype(BF16), preferred_element_type=F32)
    u = jnp.dot(x, wu_ref[...].astype(BF16), preferred_element_type=F32)
    o_ref[...] = (g * _sigmoid(g) * u).astype(o_ref.dtype)


def _gateup(x, wg, wu, layer, tm, tn):
    M, K = x.shape
    N = wg.shape[2]
    tm, tn = _pick(M, tm, SUBLANE), _pick(N, tn, LANE)
    vmem = _nbytes((tm, K), BF16) + 2 * (2 * _nbytes((K, tn), F32) + _nbytes((tm, tn), BF16))
    vmem += 4 * _nbytes((tm, tn), F32) + 2 * _nbytes((K, tn), BF16)
    wspec = pl.BlockSpec((None, K, tn), lambda i, j: (layer, 0, j))
    return pl.pallas_call(
        _gateup_body,
        grid=(M // tm, N // tn),
        in_specs=[_resident((tm, K), lambda i, j: (i, 0)), wspec, wspec],
        out_specs=pl.BlockSpec((tm, tn), lambda i, j: (i, j)),
        out_shape=jax.ShapeDtypeStruct((M, N), BF16),
        compiler_params=_params(2, vmem),
        name="swiglu_gate_up",
    )(x, wg, wu)


def _ln_body(x_ref, m_ref, g_ref, b_ref, o_ref, ob_ref, *, alpha):
    v = alpha * x_ref[...] + m_ref[...]
    mu = jnp.mean(v, axis=-1, keepdims=True)
    d = v - mu
    var = jnp.mean(d * d, axis=-1, keepdims=True)
    y = d * lax.rsqrt(var + LN_EPS) * g_ref[...] + b_ref[...]
    o_ref[...] = y
    ob_ref[...] = y.astype(BF16)


def _residual_layernorm(x, m, g, b, alpha, name):
    M, D = x.shape
    tm = _pick(M, 256, SUBLANE)
    vmem = 2 * (3 * _nbytes((tm, D), F32) + _nbytes((tm, D), BF16)) + 4 * _nbytes((tm, D), F32)
    row = pl.BlockSpec((tm, D), lambda i: (i, 0))
    vec = pl.BlockSpec((1, D), lambda i: (0, 0))
    return pl.pallas_call(
        functools.partial(_ln_body, alpha=alpha),
        grid=(M // tm,),
        in_specs=[row, row, vec, vec],
        out_specs=[row, row],
        out_shape=[jax.ShapeDtypeStruct((M, D), F32), jax.ShapeDtypeStruct((M, D), BF16)],
        compiler_params=_params(1, vmem),
        name=name,
    )(x, m, g, b)


def _softplus(z):
    return jnp.maximum(z, 0.0) + jnp.log(1.0 + jnp.exp(-jnp.abs(z)))


def _rms(o, g):
    return o * lax.rsqrt(jnp.mean(o * o, axis=-1, keepdims=True) + RMS_EPS) * g


def _lanes(x, n):
    reps = n // LANE
    return x if reps == 1 else jnp.concatenate([x] * reps, axis=1)


def _cumsum_weights():
    j = jnp.arange(LANE)[:, None]
    s = jnp.arange(LANE)[None, :]
    blk = jnp.concatenate([(j > s).astype(BF16), jnp.ones((LANE, LANE), BF16)], axis=1)
    return jnp.concatenate([blk, blk], axis=0)


def _sb_weights(z, valid, run, wcs):
    sp = _softplus(z)
    log_keep = -sp if valid is None else jnp.where(valid, -sp, 0.0)
    log_beta = z - sp
    groups = z.shape[1] // LANE
    ws = [None] * groups
    for g in reversed(range(groups)):
        sl = slice(g * LANE, (g + 1) * LANE)
        lk = log_keep[:, sl]
        hi = lk.astype(BF16)
        lo = (lk - hi.astype(F32)).astype(BF16)
        cs = jnp.dot(jnp.concatenate([hi, lo], axis=1), wcs, preferred_element_type=F32)
        ws[g] = jnp.exp(log_beta[:, sl] + cs[:, :LANE] + run)
        run = run + cs[:, LANE:]
    w = ws[0] if groups == 1 else jnp.concatenate(ws, axis=1)
    if valid is not None:
        w = jnp.where(valid, w, 0.0)
    return w, run


def _softmax_update(s, m_prev, l_prev):
    m_new = jnp.maximum(m_prev, jnp.max(s, axis=1, keepdims=True))
    alpha = jnp.exp(m_prev - m_new)
    p = jnp.exp(s - _lanes(m_new, s.shape[1]))
    return p, alpha, m_new, alpha * l_prev + jnp.sum(p, axis=1, keepdims=True)


def _df_lambda(lq1_ref, lk1_ref, lq2_ref, lk2_ref, lam_init):
    s1 = jnp.sum(lq1_ref[...] * lk1_ref[...], axis=1, keepdims=True)
    s2 = jnp.sum(lq2_ref[...] * lk2_ref[...], axis=1, keepdims=True)
    return jnp.exp(s1) - jnp.exp(s2) + lam_init


def _split_halves(q):
    first = lax.broadcasted_iota(jnp.int32, q.shape, 1) < LANE // 2
    return jnp.concatenate([jnp.where(first, q, 0.0), jnp.where(first, 0.0, q)], axis=0)


def _cast_kv(i, k_ref, v_ref, kb_ref, vb_ref):
    @pl.when(i == 0)
    def _():
        kb_ref[...] = k_ref[...].astype(BF16)
        vb_ref[...] = v_ref[...].astype(BF16)


def _sb_prompt_body(q_ref, k_ref, v_ref, g_ref, wcs_ref, o_ref, kb_ref, vb_ref, acc_ref, run_ref,
                    *, hb, tq, scale):
    i = pl.program_id(2)
    _cast_kv(i, k_ref, v_ref, kb_ref, vb_ref)
    qb = [q_ref[h].astype(BF16) for h in range(hb)]
    wcs = wcs_ref[...]
    acc_ref[...] = jnp.zeros_like(acc_ref)
    run_ref[...] = jnp.zeros_like(run_ref)

    def tile(j, valid):
        ks = pl.multiple_of(j * tq, tq)
        live = None
        for h in range(hb):
            kt = kb_ref[h, pl.ds(ks, tq), :]
            vt = vb_ref[h, pl.ds(ks, tq), :]
            z = lax.dot_general(qb[h], kt, _NT, preferred_element_type=F32) * scale
            w, run = _sb_weights(z, valid, run_ref[h], wcs)
            run_ref[h] = run
            acc_ref[h] += jnp.dot(w.astype(BF16), vt, preferred_element_type=F32)
            top = jnp.max(run)
            live = top if live is None else jnp.maximum(live, top)
        return live

    row = lax.broadcasted_iota(jnp.int32, (tq, tq), 0)
    col = lax.broadcasted_iota(jnp.int32, (tq, tq), 1)
    live = tile(i, col < row)

    def more(c):
        return (c[0] < i) & (c[1] > EXP_UNDERFLOW)

    def body(c):
        return c[0] + 1, tile(i - 1 - c[0], None)

    lax.while_loop(more, body, (jnp.int32(0), live))
    g = g_ref[...]
    for h in range(hb):
        o_ref[:, h * LANE:(h + 1) * LANE] = _rms(acc_ref[h], g).astype(o_ref.dtype)


def _df_prompt_body(q_ref, k_ref, v_ref, g_ref, lq1_ref, lk1_ref, lq2_ref, lk2_ref, o_ref,
                    kb_ref, vb_ref, m_ref, l_ref, acc_ref, *, hb, tq, scale, lam_init):
    i = pl.program_id(2)
    _cast_kv(i, k_ref, v_ref, kb_ref, vb_ref)
    qq = [_split_halves(q_ref[h]).astype(BF16) for h in range(hb)]
    m_ref[...] = jnp.full_like(m_ref, MASK_NEG)
    l_ref[...] = jnp.zeros_like(l_ref)
    acc_ref[...] = jnp.zeros_like(acc_ref)

    def tile(j, valid):
        ks = pl.multiple_of(j * tq, tq)
        for h in range(hb):
            kt = kb_ref[h, pl.ds(ks, tq), :]
            vt = vb_ref[h, pl.ds(ks, tq), :]
            s = lax.dot_general(qq[h], kt, _NT, preferred_element_type=F32) * scale
            if valid is not None:
                s = jnp.where(valid, s, MASK_NEG)
            p, alpha, m_ref[h], l_ref[h] = _softmax_update(s, m_ref[h], l_ref[h])
            acc_ref[h] = acc_ref[h] * alpha + jnp.dot(p.astype(BF16), vt, preferred_element_type=F32)

    row = lax.broadcasted_iota(jnp.int32, (2 * tq, tq), 0)
    col = lax.broadcasted_iota(jnp.int32, (2 * tq, tq), 1)
    row = jnp.where(row >= tq, row - tq, row)
    tile(i, col <= row)

    def body(jj, c):
        tile(jj, None)
        return c

    lax.fori_loop(0, i, body, 0)
    g = g_ref[...]
    lam = _df_lambda(lq1_ref, lk1_ref, lq2_ref, lk2_ref, lam_init)
    for h in range(hb):
        o = acc_ref[h] / l_ref[h]
        o = o[:tq] - lam * o[tq:]
        o_ref[:, h * LANE:(h + 1) * LANE] = (_rms(o, g) * (1.0 - lam_init)).astype(o_ref.dtype)


def _prompt_attention(kind, proj, kv_k, kv_v, layer, col0, norm_g, extra, tq, hb):
    batch, _, seq, _ = proj.shape
    heads = kv_k.shape[2]
    tq = _pick(seq, tq, LANE)
    nq = seq // tq
    hb = _heads_per_step(heads, col0, hb)
    qspec = pl.BlockSpec((None, hb, tq, LANE), lambda b, h, i: (b, col0 // hb + h, i, 0))
    kvspec = pl.BlockSpec((None, None, hb, seq, LANE), lambda b, h, i: (layer, b, h, 0, 0))
    const = lambda shape: pl.BlockSpec(shape, lambda b, h, i: (0,) * len(shape))
    kv_scratch = [pltpu.VMEM((hb, seq, LANE), BF16), pltpu.VMEM((hb, seq, LANE), BF16)]
    vmem = hb * (4 * _nbytes((seq, LANE), F32) + 4 * _nbytes((tq, LANE), F32) + 2 * _nbytes((seq, LANE), BF16))
    vmem += hb * 16 * _nbytes((2 * tq, tq), F32)
    if kind == "sb":
        body = functools.partial(_sb_prompt_body, hb=hb, tq=tq, scale=LANE ** -0.5)
        ins = [proj, kv_k, kv_v, norm_g, _cumsum_weights()]
        specs = [qspec, kvspec, kvspec, const((1, LANE)), const((2 * LANE, 2 * LANE))]
        scratch = kv_scratch + [pltpu.VMEM((hb, tq, LANE), F32)] * 2
    else:
        lam_init, lq1, lk1, lq2, lk2 = extra
        body = functools.partial(_df_prompt_body, hb=hb, tq=tq, scale=(LANE // 2) ** -0.5, lam_init=lam_init)
        ins = [proj, kv_k, kv_v, norm_g, lq1, lk1, lq2, lk2]
        specs = [qspec, kvspec, kvspec, const((1, LANE))] + [const((1, LANE // 2))] * 4
        scratch = kv_scratch + [pltpu.VMEM((hb, 2 * tq, LANE), F32)] * 3
    return pl.pallas_call(
        body,
        grid=(batch, heads // hb, nq),
        in_specs=specs,
        out_specs=pl.BlockSpec((tq, hb * LANE), lambda b, h, i: (b * nq + i, h)),
        out_shape=jax.ShapeDtypeStruct((batch * seq, heads * LANE), BF16),
        scratch_shapes=scratch,
        compiler_params=_params(3, vmem),
        name=kind + "_attention_prompt",
    )(*ins)


def _hgrn_body(q_ref, f_ref, i_ref, gate_ref, lbp_ref, gn_ref, lmat_ref, s0_ref, o_ref, s_ref,
               st_ref, qa_ref, key_ref, bl_ref, oraw_ref, *, hb, rows, layer, depth, t_valid, nt):
    t = pl.program_id(2)

    @pl.when(t == 0)
    def _():
        for h in range(hb):
            st_ref[h] = s0_ref[h].T

    lmat = lmat_ref[...]
    for h in range(hb):
        lbp = lbp_ref[:, h * LANE:(h + 1) * LANE]
        prow = [lbp[r:r + 1] for r in range(depth)]
        mx = functools.reduce(jnp.maximum, prow)
        es = [jnp.exp(r - mx) for r in prow]
        lb = sum(es[1:layer + 1], jnp.zeros_like(mx)) / sum(es[1:], es[0])
        p = f_ref[h]
        log_f = -_softplus(-p) + jnp.log1p(lb * jnp.exp(-p))
        key = (1.0 - lb) * (1.0 / (1.0 + jnp.exp(p)))
        if t_valid is not None:
            ok = (t * rows + lax.broadcasted_iota(jnp.int32, p.shape, 0)) < t_valid
            log_f = jnp.where(ok, log_f, 0.0)
            key = jnp.where(ok, key, 0.0)
        hi = log_f.astype(BF16)
        r1 = log_f - hi.astype(F32)
        mid = r1.astype(BF16)
        lo = (r1 - mid.astype(F32)).astype(BF16)
        bl_ref[h] = jnp.dot(lmat, jnp.concatenate([hi, mid, lo], axis=0), preferred_element_type=F32)
        qv = q_ref[h]
        qa_ref[h] = qv * _sigmoid(qv)
        key_ref[h] = key

    rowi = lax.broadcasted_iota(jnp.int32, (HG_CHUNK, LANE), 0)

    def chunk(c, carry):
        r0 = pl.multiple_of(c * HG_CHUNK, HG_CHUNK)
        for h in range(hb):
            blc = bl_ref[h, pl.ds(r0, HG_CHUNK), :]
            qac = qa_ref[h, pl.ds(r0, HG_CHUNK), :]
            kc = key_ref[h, pl.ds(r0, HG_CHUNK), :]
            vc = i_ref[h, pl.ds(r0, HG_CHUNK), :]
            blast = blc[HG_CHUNK - 1:HG_CHUNK]
            st = st_ref[h]
            o = lax.dot_general((qac * jnp.exp(blc)).astype(BF16), st.astype(BF16), _NT,
                                preferred_element_type=F32)
            for s in range(HG_CHUNK):
                e = jnp.exp(jnp.where(rowi >= s, blc - blc[s:s + 1], MASK_NEG))
                cv = jnp.sum(qac * e * kc[s:s + 1], axis=1, keepdims=True)
                o = o + cv * vc[s:s + 1]
            oraw_ref[h, pl.ds(r0, HG_CHUNK), :] = o
            kt = kc * jnp.exp(blast - blc)
            upd = lax.dot_general(vc.astype(BF16), kt.astype(BF16), _TN, preferred_element_type=F32)
            st_ref[h] = st * jnp.exp(blast) + upd
        return carry

    lax.fori_loop(0, rows // HG_CHUNK, chunk, 0)
    gn = gn_ref[...]
    for h in range(hb):
        gate = gate_ref[h]
        o_ref[:, h * LANE:(h + 1) * LANE] = (_rms(oraw_ref[h], gn) * (gate * _sigmoid(gate))).astype(o_ref.dtype)

    @pl.when(t == nt - 1)
    def _():
        for h in range(hb):
            s_ref[h] = st_ref[h].T


def _hgrn(proj, heads, col0, hg_lb, norm_g, s0, layer, t_valid, hb):
    batch, _, seq, _ = proj.shape
    rows = _pick(seq, 256, HG_CHUNK)
    nt = seq // rows
    depth = hg_lb.shape[0]
    hb = _heads_per_step(heads, col0, hb)
    r = jnp.arange(rows)
    lmat = ((r[:, None] // HG_CHUNK == r[None, :] // HG_CHUNK) & (r[None, :] <= r[:, None])).astype(BF16)
    lmat = jnp.concatenate([lmat] * 3, axis=1)

    def col(k):
        return pl.BlockSpec((None, hb, rows, LANE), lambda b, h, t: (b, (col0 + k * heads) // hb + h, t, 0))

    state = pl.BlockSpec((None, hb, LANE, LANE), lambda b, h, t: (b, h, 0, 0))
    vmem = hb * (2 * (5 * _nbytes((rows, LANE), F32) + 4 * _nbytes((LANE, LANE), F32)) + 16 * _nbytes((rows, LANE), F32))
    vmem += 2 * _nbytes((rows, 3 * rows), BF16)
    return pl.pallas_call(
        functools.partial(_hgrn_body, hb=hb, rows=rows, layer=layer, depth=depth,
                          t_valid=None if t_valid == seq else t_valid, nt=nt),
        grid=(batch, heads // hb, nt),
        in_specs=[col(0), col(1), col(2), col(3),
                  pl.BlockSpec((depth, hb * LANE), lambda b, h, t: (0, h)),
                  pl.BlockSpec((1, LANE), lambda b, h, t: (0, 0)),
                  pl.BlockSpec((rows, 3 * rows), lambda b, h, t: (0, 0)),
                  state],
        out_specs=[pl.BlockSpec((rows, hb * LANE), lambda b, h, t: (b * nt + t, h)), state],
        out_shape=[jax.ShapeDtypeStruct((batch * seq, heads * LANE), BF16),
                   jax.ShapeDtypeStruct(s0.shape, F32)],
        scratch_shapes=[pltpu.VMEM((hb, LANE, LANE), F32)] + [pltpu.VMEM((hb, rows, LANE), F32)] * 4,
        compiler_params=_params(3, vmem),
        name="hgrn2",
    )(proj, proj, proj, proj, hg_lb, norm_g, lmat, s0)


def _head_scores(qb, k_refs, heads, rows_per_head):
    out = []
    for h in range(heads):
        kh = [r[h].astype(BF16) for r in k_refs]
        kh = kh[0] if len(kh) == 1 else jnp.concatenate(kh, axis=0)
        out.append(lax.dot_general(qb[h * rows_per_head:(h + 1) * rows_per_head], kh, _NT,
                                   preferred_element_type=F32))
    return jnp.concatenate(out, axis=0)


def _head_values(w, v_refs, heads, rows_per_head):
    out = []
    for h in range(heads):
        vh = [r[h].astype(BF16) for r in v_refs]
        vh = vh[0] if len(vh) == 1 else jnp.concatenate(vh, axis=0)
        out.append(jnp.dot(w[h * rows_per_head:(h + 1) * rows_per_head].astype(BF16), vh,
                           preferred_element_type=F32))
    return jnp.concatenate(out, axis=0)


def _write_heads(o_ref, head_fn, heads):
    pad = jnp.zeros((o_ref.shape[0] - Q_SLOTS, LANE), F32)
    for h in range(heads):
        o_ref[:, h * LANE:(h + 1) * LANE] = jnp.concatenate([head_fn(h), pad], axis=0).astype(o_ref.dtype)


def _sb_decode_head_body(pt_ref, q_ref, kn_ref, vn_ref, *rest, group, heads, scale):
    k_refs, v_refs = rest[:group], rest[group:2 * group]
    wcs_ref, acc_ref, run_ref, live_ref = rest[2 * group:]
    p = pl.program_id(1)
    qb = q_ref[...]
    wcs = wcs_ref[...]

    @pl.when(p == 0)
    def _():
        z = _head_scores(qb, [kn_ref], heads, Q_SLOTS) * scale
        tok = lax.broadcasted_iota(jnp.int32, z.shape, 0) & (Q_SLOTS - 1)
        col = lax.broadcasted_iota(jnp.int32, z.shape, 1)
        w, run = _sb_weights(z, col < tok, jnp.zeros(run_ref.shape, F32), wcs)
        run_ref[...] = run
        acc_ref[...] = _head_values(w, [vn_ref], heads, Q_SLOTS)

    @pl.when(p == 1)
    def _():
        z = _head_scores(qb, k_refs, heads, Q_SLOTS) * scale
        w, run = _sb_weights(z, None, run_ref[...], wcs)
        run_ref[...] = run
        acc_ref[...] += _head_values(w, v_refs, heads, Q_SLOTS)
        live_ref[...] = jnp.full(live_ref.shape, 1, jnp.int32) * (jnp.max(run) > EXP_UNDERFLOW).astype(jnp.int32)


def _sb_decode_tail_body(pt_ref, live_ref, q_ref, *rest, group, heads, scale):
    k_refs, v_refs = rest[:group], rest[group:2 * group]
    g_ref, wcs_ref, acc0_ref, run0_ref, o_ref, acc_ref, run_ref = rest[2 * group:]
    b, s = pl.program_id(0), pl.program_id(1)

    @pl.when(s == 0)
    def _():
        acc_ref[...] = acc0_ref[...]
        run_ref[...] = run0_ref[...]

    @pl.when(live_ref[b] > 0)
    def _():
        z = _head_scores(q_ref[...], k_refs, heads, Q_SLOTS) * scale
        w, run = _sb_weights(z, None, run_ref[...], wcs_ref[...])
        run_ref[...] = run
        acc_ref[...] += _head_values(w, v_refs, heads, Q_SLOTS)

    @pl.when(s == pl.num_programs(1) - 1)
    def _():
        g = g_ref[...]
        _write_heads(o_ref, lambda h: _rms(acc_ref[h * Q_SLOTS:(h + 1) * Q_SLOTS, :], g), heads)


def _df_decode_body(pt_ref, q_ref, kn_ref, vn_ref, *rest, group, heads, scale, lam_init):
    k_refs, v_refs = rest[:group], rest[group:2 * group]
    g_ref, lq1_ref, lk1_ref, lq2_ref, lk2_ref, o_ref, m_ref, l_ref, acc_ref = rest[2 * group:]
    p = pl.program_id(1)
    qb = q_ref[...]
    rph = 2 * Q_SLOTS

    def step(s, v_parts, first):
        m_prev = jnp.full(m_ref.shape, MASK_NEG, F32) if first else m_ref[...]
        l_prev = jnp.zeros(l_ref.shape, F32) if first else l_ref[...]
        pr, alpha, m_ref[...], l_ref[...] = _softmax_update(s, m_prev, l_prev)
        pv = _head_values(pr, v_parts, heads, rph)
        acc_ref[...] = pv if first else acc_ref[...] * alpha + pv

    @pl.when(p == 0)
    def _():
        s = _head_scores(qb, [kn_ref], heads, rph) * scale
        tok = lax.broadcasted_iota(jnp.int32, s.shape, 0) & (Q_SLOTS - 1)
        col = lax.broadcasted_iota(jnp.int32, s.shape, 1)
        step(jnp.where(col <= tok, s, MASK_NEG), [vn_ref], True)

    @pl.when(p > 0)
    def _():
        step(_head_scores(qb, k_refs, heads, rph) * scale, v_refs, False)

    @pl.when(p == pl.num_programs(1) - 1)
    def _():
        g = g_ref[...]
        lam = _df_lambda(lq1_ref, lk1_ref, lq2_ref, lk2_ref, lam_init)

        def head(h):
            r1 = slice(h * rph, h * rph + Q_SLOTS)
            r2 = slice(h * rph + Q_SLOTS, (h + 1) * rph)
            o = acc_ref[r1, :] / l_ref[r1, :] - lam * (acc_ref[r2, :] / l_ref[r2, :])
            return _rms(o, g) * (1.0 - lam_init)

        _write_heads(o_ref, head, heads)


def _decode_inputs(q, k_new, v_new, page, split):
    B, heads, tq, _ = q.shape
    pad = ((0, 0), (0, 0), (0, page - tq), (0, 0))
    k_new, v_new = jnp.pad(k_new, pad), jnp.pad(v_new, pad)
    q = jnp.pad(q, ((0, 0), (0, 0), (0, Q_SLOTS - tq), (0, 0)))
    if split:
        first = jnp.arange(LANE) < LANE // 2
        q = jnp.stack([jnp.where(first, q, 0.0), jnp.where(first, 0.0, q)], axis=2)
    return q.reshape(B, -1, LANE).astype(BF16), k_new, v_new


def _page_specs(layer, heads, page, group, page_of):
    def spec(gi):
        return pl.BlockSpec((None, None, heads, page, LANE),
                            lambda b, s, *pre: (layer, page_of(gi, b, s, *pre), 0, 0, 0))
    return [spec(gi) for gi in range(group)]


def _decode_vmem(heads, page, group, qrows):
    vmem = 2 * (2 * group + 2) * _nbytes((heads, page, LANE), F32)
    return vmem + 4 * _nbytes((heads, group * page, LANE), BF16) + 24 * _nbytes((qrows, group * page), F32)


def _sb_decode_attention(q, k_new, v_new, pool_k, pool_v, page_table, layer, norm_g, out_rows, group):
    B, heads, _, _ = q.shape
    page = pool_k.shape[3]
    n_pages = page_table.shape[1]
    group = _pick(n_pages // 2, group, 1)
    n_groups = n_pages // group
    qb, k_new, v_new = _decode_inputs(q, k_new, v_new, page, False)
    qrows = heads * Q_SLOTS
    wcs = _cumsum_weights()
    scale = LANE ** -0.5
    vmem = _decode_vmem(heads, page, group, qrows)
    per_b = lambda shape: pl.BlockSpec((None,) + shape, lambda b, s, *pre: (b,) + (0,) * len(shape))
    const = lambda shape: pl.BlockSpec(shape, lambda b, s, *pre: (0,) * len(shape))
    state = jax.ShapeDtypeStruct((B, qrows, LANE), F32)

    last = _page_specs(layer, heads, page, group, lambda gi, b, s, pt: pt[b, (n_groups - 1) * group + gi])
    acc, run, live = pl.pallas_call(
        functools.partial(_sb_decode_head_body, group=group, heads=heads, scale=scale),
        grid_spec=pltpu.PrefetchScalarGridSpec(
            num_scalar_prefetch=1,
            grid=(B, 2),
            in_specs=[per_b((qrows, LANE)), per_b((heads, page, LANE)), per_b((heads, page, LANE))]
            + last + last + [const((2 * LANE, 2 * LANE))],
            out_specs=[per_b((qrows, LANE)), per_b((qrows, LANE)), per_b((SUBLANE, LANE))],
        ),
        out_shape=[state, state, jax.ShapeDtypeStruct((B, SUBLANE, LANE), jnp.int32)],
        compiler_params=_params(2, vmem),
        name="sb_attention_decode_head",
    )(page_table, qb, k_new, v_new, *([pool_k] * group), *([pool_v] * group), wcs)
    live = live[:, 0, 0]

    def earlier(gi, b, s, pt, lv):
        grp = jnp.where(lv[b] > 0, n_groups - 2 - s, 0)
        return pt[b, grp * group + gi]

    pages = _page_specs(layer, heads, page, group, earlier)
    return pl.pallas_call(
        functools.partial(_sb_decode_tail_body, group=group, heads=heads, scale=scale),
        grid_spec=pltpu.PrefetchScalarGridSpec(
            num_scalar_prefetch=2,
            grid=(B, n_groups - 1),
            in_specs=[per_b((qrows, LANE))] + pages + pages
            + [const((1, LANE)), const((2 * LANE, 2 * LANE)), per_b((qrows, LANE)), per_b((qrows, LANE))],
            out_specs=per_b((out_rows, heads * LANE)),
            scratch_shapes=[pltpu.VMEM((qrows, LANE), F32), pltpu.VMEM((qrows, LANE), F32)],
        ),
        out_shape=jax.ShapeDtypeStruct((B, out_rows, heads * LANE), BF16),
        compiler_params=_params(2, vmem),
        name="sb_attention_decode_tail",
    )(page_table, live, qb, *([pool_k] * group), *([pool_v] * group), norm_g, wcs, acc, run)


def _df_decode_attention(q, k_new, v_new, pool_k, pool_v, page_table, layer, norm_g, extra, out_rows, group):
    B, heads, _, _ = q.shape
    page = pool_k.shape[3]
    n_pages = page_table.shape[1]
    group = _pick(n_pages, group, 1)
    n_groups = n_pages // group
    qb, k_new, v_new = _decode_inputs(q, k_new, v_new, page, True)
    qrows = heads * 2 * Q_SLOTS
    lam_init, lq1, lk1, lq2, lk2 = extra
    per_b = lambda shape: pl.BlockSpec((None,) + shape, lambda b, p, pt: (b,) + (0,) * len(shape))
    const = lambda shape: pl.BlockSpec(shape, lambda b, p, pt: (0,) * len(shape))
    pages = _page_specs(layer, heads, page, group,
                        lambda gi, b, p, pt: pt[b, (n_groups - jnp.maximum(p, 1)) * group + gi])
    return pl.pallas_call(
        functools.partial(_df_decode_body, group=group, heads=heads, scale=(LANE // 2) ** -0.5,
                          lam_init=lam_init),
        grid_spec=pltpu.PrefetchScalarGridSpec(
            num_scalar_prefetch=1,
            grid=(B, n_groups + 1),
            in_specs=[per_b((qrows, LANE)), per_b((heads, page, LANE)), per_b((heads, page, LANE))]
            + pages + pages + [const((1, LANE))] + [const((1, LANE // 2))] * 4,
            out_specs=per_b((out_rows, heads * LANE)),
            scratch_shapes=[pltpu.VMEM((qrows, LANE), F32)] * 3,
        ),
        out_shape=jax.ShapeDtypeStruct((B, out_rows, heads * LANE), BF16),
        compiler_params=_params(2, _decode_vmem(heads, page, group, qrows)),
        name="df_attention_decode",
    )(page_table, qb, k_new, v_new, *([pool_k] * group), *([pool_v] * group), norm_g, lq1, lk1, lq2, lk2)


def _layer(x, xb, batch, seq, t_valid, layer, w, kv_prev, s_hg_past, cache, page_table, dims):
    h_sb, h_hg, h_df = dims
    depth = w["in"].shape[0]
    alpha = (2 * depth) ** 0.25
    lam_init = 0.8 - 0.6 * math.exp(-0.3 * layer)
    c_hg, c_df = 3 * h_sb, 3 * h_sb + 4 * h_hg
    kv_cols = ((h_sb, h_sb), (2 * h_sb, h_sb), (c_df + h_df, h_df), (c_df + 2 * h_df, h_df))
    proj, kv = _in_projection(xb, w["in"], layer, batch, seq, kv_cols, kv_prev)
    row = lambda name: w[name][layer][None, :]
    df_extra = (lam_init, row("lq1"), row("lk1"), row("lq2"), row("lk2"))
    if cache is None:
        o_sb = _prompt_attention("sb", proj, kv[0], kv[1], layer, 0, row("sb_g"), None, 256, 4)
        o_df = _prompt_attention("df", proj, kv[2], kv[3], layer, c_df, row("df_g"), df_extra, 256, 4)
    else:
        new = lambda a: a[layer, :, :, :t_valid]
        o_sb = _sb_decode_attention(proj[:, :h_sb, :t_valid], new(kv[0]), new(kv[1]), cache[0], cache[1],
                                    page_table, layer, row("sb_g"), seq, 8).reshape(batch * seq, -1)
        o_df = _df_decode_attention(proj[:, c_df:c_df + h_df, :t_valid], new(kv[2]), new(kv[3]), cache[2],
                                    cache[3], page_table, layer, row("df_g"), df_extra, seq, 4
                                    ).reshape(batch * seq, -1)
    o_hg, s_hg = _hgrn(proj, h_hg, c_hg, w["hg_lb"], row("hg_g"), s_hg_past, layer, t_valid, 4)
    m = _out_projection([o_sb, o_hg, o_df], w["out"], layer, 2048, 512)
    h, hb = _residual_layernorm(x, m, row("ln1_g"), row("ln1_b"), alpha, "layernorm_mixer")
    hid = _gateup(hb, w["gate"], w["up"], layer, 2048, 256)
    f = _matmul(hid, w["down"], layer, 1024, 256, "swiglu_down")
    y, yb = _residual_layernorm(h, f, row("ln2_g"), row("ln2_b"), alpha, "layernorm_ffn")
    return y, yb, kv, s_hg


def kernel(x_prompt, x_sample, cache_sb_k, cache_sb_v, cache_df_k, cache_df_v, state_hg, page_table, w_in, w_out, sb_norm_g, hg_lb, hg_norm_g, df_lq1, df_lk1, df_lq2, df_lk2, df_norm_g, ln1_g, ln1_b, w_gate, w_up, w_down, ln2_g, ln2_b):
    depth = w_in.shape[0]
    bp, seq, d_model = x_prompt.shape
    bs, dec_seq, _ = x_sample.shape
    dims = (cache_sb_k.shape[3], state_hg.shape[2], cache_df_k.shape[3])
    assert dec_seq <= Q_SLOTS and cache_sb_k.shape[4] == LANE and state_hg.shape[3:] == (LANE, LANE)
    dec_rows = HG_CHUNK
    w = {"in": w_in, "out": w_out, "gate": w_gate, "up": w_up,
         "down": w_down.astype(BF16),
         "sb_g": sb_norm_g, "hg_g": hg_norm_g, "df_g": df_norm_g, "hg_lb": hg_lb,
         "lq1": df_lq1, "lk1": df_lk1, "lq2": df_lq2, "lk2": df_lk2,
         "ln1_g": ln1_g, "ln1_b": ln1_b, "ln2_g": ln2_g, "ln2_b": ln2_b}

    xp = x_prompt.reshape(bp * seq, d_model)
    xs = jnp.pad(x_sample, ((0, 0), (0, dec_rows - dec_seq), (0, 0))).reshape(bs * dec_rows, d_model)
    xpb, xsb = xp.astype(BF16), xs.astype(BF16)
    cache = tuple(jnp.transpose(c, (0, 1, 3, 2, 4)) for c in (cache_sb_k, cache_sb_v, cache_df_k, cache_df_v))
    s_zero = jnp.zeros((bp,) + state_hg.shape[2:], state_hg.dtype)
    kv_p, kv_s, st_p, st_s = (), (), [], []
    for layer in range(depth):
        xp, xpb, kv_p, s = _layer(xp, xpb, bp, seq, seq, layer, w, kv_p, s_zero, None, None, dims)
        st_p.append(s)
        xs, xsb, kv_s, s = _layer(xs, xsb, bs, dec_rows, dec_seq, layer, w, kv_s, state_hg[layer], cache,
                                  page_table, dims)
        st_s.append(s)
    y_prompt = xp.reshape(bp, seq, d_model)
    y_sample = xs.reshape(bs, dec_rows, d_model)[:, :dec_seq]
    tokens_major = lambda a, t: jnp.transpose(a[:, :, :, :t], (0, 1, 3, 2, 4))
    return (y_prompt, y_sample,
            *(tokens_major(a, seq) for a in kv_p), jnp.stack(st_p),
            *(tokens_major(a, dec_seq) for a in kv_s), jnp.stack(st_s))
```

```python
import functools
import math

import jax
import jax.numpy as jnp
from jax import lax
from jax.experimental import pallas as pl
from jax.experimental.pallas import tpu as pltpu

F32 = jnp.float32
BF16 = jnp.bfloat16

LANE = 128
SUBLANE = 8
VMEM_CAP = 60000 * 1024
HG_CHUNK = 16
Q_SLOTS = SUBLANE
LN_EPS = 1e-5
RMS_EPS = 1e-6
MASK_NEG = -1e30
EXP_UNDERFLOW = -104.0

_NT = (((1,), (1,)), ((), ()))
_TN = (((0,), (0,)), ((), ()))


def _params(n_axes, vmem_bytes):
    limit = int(min(VMEM_CAP, max(32 * 1024 * 1024, vmem_bytes)))
    return pltpu.CompilerParams(dimension_semantics=("arbitrary",) * n_axes, vmem_limit_bytes=limit)


def _pick(dim, pref, align):
    if dim <= pref:
        return dim
    t = (pref // align) * align
    while t >= align:
        if dim % t == 0:
            return t
        t -= align
    return dim


def _heads_per_step(heads, col0, pref):
    hb = pref
    while heads % hb or col0 % hb:
        hb -= 1
    return hb


def _nbytes(shape, dtype):
    return math.prod(shape) * jnp.dtype(dtype).itemsize


def _resident(shape, index):
    return pl.BlockSpec(shape, index, pipeline_mode=pl.Buffered(1))


def _mm_body(x_ref, w_ref, o_ref):
    o_ref[...] = jnp.dot(x_ref[...], w_ref[...].astype(BF16), preferred_element_type=F32)


def _matmul(x, w, layer, tm, tn, name):
    M, K = x.shape
    N = w.shape[2]
    tm, tn = _pick(M, tm, SUBLANE), _pick(N, tn, LANE)
    vmem = _nbytes((tm, K), BF16) + 2 * (_nbytes((K, tn), w.dtype) + _nbytes((tm, tn), F32))
    vmem += 3 * _nbytes((tm, tn), F32) + _nbytes((K, tn), BF16)
    return pl.pallas_call(
        _mm_body,
        grid=(M // tm, N // tn),
        in_specs=[_resident((tm, K), lambda i, j: (i, 0)),
                  pl.BlockSpec((None, K, tn), lambda i, j: (layer, 0, j))],
        out_specs=pl.BlockSpec((tm, tn), lambda i, j: (i, j)),
        out_shape=jax.ShapeDtypeStruct((M, N), F32),
        compiler_params=_params(2, vmem),
        name=name,
    )(x, w)


def _out_proj_body(*refs, widths):
    x_refs = refs[:len(widths)]
    w_ref, o_ref = refs[len(widths):]
    acc, r0 = None, 0
    for x_ref, kw in zip(x_refs, widths):
        part = jnp.dot(x_ref[...], w_ref[r0:r0 + kw, :].astype(BF16), preferred_element_type=F32)
        acc = part if acc is None else acc + part
        r0 += kw
    o_ref[...] = acc


def _out_projection(xs, w, layer, tm, tn):
    M = xs[0].shape[0]
    widths = tuple(x.shape[1] for x in xs)
    K, N = w.shape[1:]
    assert sum(widths) == K
    tm, tn = _pick(M, tm, SUBLANE), _pick(N, tn, LANE)
    vmem = _nbytes((tm, K), BF16) + 2 * (_nbytes((K, tn), F32) + _nbytes((tm, tn), F32))
    vmem += 3 * _nbytes((tm, tn), F32) + _nbytes((K, tn), BF16)
    return pl.pallas_call(
        functools.partial(_out_proj_body, widths=widths),
        grid=(M // tm, N // tn),
        in_specs=[_resident((tm, kw), lambda i, j: (i, 0)) for kw in widths]
        + [pl.BlockSpec((None, K, tn), lambda i, j: (layer, 0, j))],
        out_specs=pl.BlockSpec((tm, tn), lambda i, j: (i, j)),
        out_shape=jax.ShapeDtypeStruct((M, N), F32),
        compiler_params=_params(2, vmem),
        name="out_projection",
    )(*xs, w)


def _in_proj_body(x_ref, w_ref, *refs, kv_tiles):
    o_ref, *kv_refs = refs[len(refs) - 1 - len(kv_tiles):]
    res = jnp.dot(x_ref[...], w_ref[...].astype(BF16), preferred_element_type=F32)
    nb, cols, rows, _ = o_ref.shape

    for b in range(nb):
        for c in range(cols):
            o_ref[b, c] = res[b * rows:(b + 1) * rows, c * LANE:(c + 1) * LANE]
    j = pl.program_id(1)
    for (j0, n), ref in zip(kv_tiles, kv_refs):

        @pl.when((j >= j0) & (j < j0 + n))
        def _(ref=ref):
            ref[...] = o_ref[...]


def _in_projection(x, w, layer, batch, seq, kv_cols, kv_prev):
    M, K = x.shape
    depth, _, N = w.shape
    cols = N // LANE
    kc = 2
    while cols % kc or any(c0 % kc or n % kc for c0, n in kv_cols):
        kc //= 2
    tn = kc * LANE
    if seq >= 1024:
        nb, rows = 1, _pick(seq, 2048, SUBLANE)
    else:
        nb, rows = _pick(batch, max(1, 1024 // seq), 1), seq
    tm = nb * rows
    per_seq = seq // rows
    where = (lambda i: (i // per_seq, i % per_seq)) if nb == 1 else (lambda i: (i, 0))
    kv_tiles = tuple((c0 // kc, n // kc) for c0, n in kv_cols)

    def kv_spec(j0, n):
        def index(i, j):
            b, r = where(i)
            return (layer, b, jnp.clip(j - j0, 0, n - 1), r, 0)
        return pl.BlockSpec((None, nb, kc, rows, LANE), index)

    def proj_index(i, j):
        b, r = where(i)
        return (b, j, r, 0)

    n_prev = len(kv_prev)
    vmem = _nbytes((tm, K), BF16) + 2 * _nbytes((K, tn), F32) + _nbytes((K, tn), BF16)
    vmem += (2 * (1 + len(kv_cols)) + 5) * _nbytes((tm, tn), F32)
    outs = pl.pallas_call(
        functools.partial(_in_proj_body, kv_tiles=kv_tiles),
        grid=(M // tm, cols // kc),
        in_specs=[_resident((tm, K), lambda i, j: (i, 0)),
                  pl.BlockSpec((None, K, tn), lambda i, j: (layer, 0, j))]
        + [pl.BlockSpec(memory_space=pl.ANY)] * n_prev,
        out_specs=[pl.BlockSpec((nb, kc, rows, LANE), proj_index)] + [kv_spec(j0, n) for j0, n in kv_tiles],
        out_shape=[jax.ShapeDtypeStruct((batch, cols, seq, LANE), F32)]
        + [jax.ShapeDtypeStruct((depth, batch, n, seq, LANE), F32) for _, n in kv_cols],
        input_output_aliases={2 + a: 1 + a for a in range(n_prev)},
        compiler_params=_params(2, vmem),
        name="in_projection",
    )(x, w, *kv_prev)
    return outs[0], tuple(outs[1:])


def _sigmoid(x):
    return 1.0 / (1.0 + jnp.exp(-x))


def _gateup_body(x_ref, xs_ref, wg_ref, wu_ref, o_ref, os_ref):
    wg, wu = wg_ref[...].astype(BF16), wu_ref[...].astype(BF16)

    def swiglu(x):
        g = jnp.dot(x, wg, preferred_element_type=F32)
        u = jnp.dot(x, wu, preferred_element_type=F32)
        return (g * _sigmoid(g) * u).astype(BF16)

    o_ref[...] = swiglu(x_ref[...])

    @pl.when(pl.program_id(0) == 0)
    def _():
        os_ref[...] = swiglu(xs_ref[...])


def _gateup(x, xs, wg, wu, layer, tm, tn):
    M, K = x.shape
    Ms = xs.shape[0]
    N = wg.shape[2]
    tm, tn = _pick(M, tm, SUBLANE), _pick(N, tn, LANE)
    nj = N // tn
    vmem = _nbytes((tm, K), BF16) + 2 * (2 * _nbytes((K, tn), F32) + _nbytes((tm, tn), BF16))
    vmem += 4 * _nbytes((tm, tn), F32) + 2 * _nbytes((K, tn), BF16) + 4 * _nbytes((Ms, K), BF16)
    wspec = pl.BlockSpec((None, K, tn), lambda i, j: (layer, 0, j))
    small_out = pl.BlockSpec((Ms, tn), lambda i, j: (0, jnp.where(i == 0, j, nj - 1)))
    return pl.pallas_call(
        _gateup_body,
        grid=(M // tm, nj),
        in_specs=[_resident((tm, K), lambda i, j: (i, 0)), pl.BlockSpec((Ms, K), lambda i, j: (0, 0)),
                  wspec, wspec],
        out_specs=[pl.BlockSpec((tm, tn), lambda i, j: (i, j)), small_out],
        out_shape=[jax.ShapeDtypeStruct((M, N), BF16), jax.ShapeDtypeStruct((Ms, N), BF16)],
        compiler_params=_params(2, vmem),
        name="swiglu_gate_up",
    )(x, xs, wg, wu)


def _ln_body(x_ref, m_ref, g_ref, b_ref, o_ref, ob_ref, *, alpha):
    v = alpha * x_ref[...] + m_ref[...]
    mu = jnp.mean(v, axis=-1, keepdims=True)
    d = v - mu
    var = jnp.mean(d * d, axis=-1, keepdims=True)
    y = d * lax.rsqrt(var + LN_EPS) * g_ref[...] + b_ref[...]
    o_ref[...] = y
    ob_ref[...] = y.astype(BF16)


def _residual_layernorm(x, m, g, b, alpha, name):
    M, D = x.shape
    tm = _pick(M, 256, SUBLANE)
    vmem = 2 * (3 * _nbytes((tm, D), F32) + _nbytes((tm, D), BF16)) + 4 * _nbytes((tm, D), F32)
    row = pl.BlockSpec((tm, D), lambda i: (i, 0))
    vec = pl.BlockSpec((1, D), lambda i: (0, 0))
    return pl.pallas_call(
        functools.partial(_ln_body, alpha=alpha),
        grid=(M // tm,),
        in_specs=[row, row, vec, vec],
        out_specs=[row, row],
        out_shape=[jax.ShapeDtypeStruct((M, D), F32), jax.ShapeDtypeStruct((M, D), BF16)],
        compiler_params=_params(1, vmem),
        name=name,
    )(x, m, g, b)


def _softplus(z):
    return jnp.maximum(z, 0.0) + jnp.log(1.0 + jnp.exp(-jnp.abs(z)))


def _rms(o, g):
    return o * lax.rsqrt(jnp.mean(o * o, axis=-1, keepdims=True) + RMS_EPS) * g


def _lanes(x, n):
    reps = n // LANE
    return x if reps == 1 else jnp.concatenate([x] * reps, axis=1)


def _cumsum_weights():
    j = jnp.arange(LANE)[:, None]
    s = jnp.arange(LANE)[None, :]
    blk = jnp.concatenate([(j > s).astype(BF16), jnp.ones((LANE, LANE), BF16)], axis=1)
    return jnp.concatenate([blk, blk], axis=0)


def _sb_weights(z, valid, run, wcs):
    sp = _softplus(z)
    log_keep = -sp if valid is None else jnp.where(valid, -sp, 0.0)
    log_beta = z - sp
    groups = z.shape[1] // LANE
    ws = [None] * groups
    for g in reversed(range(groups)):
        sl = slice(g * LANE, (g + 1) * LANE)
        lk = log_keep[:, sl]
        hi = lk.astype(BF16)
        lo = (lk - hi.astype(F32)).astype(BF16)
        cs = jnp.dot(jnp.concatenate([hi, lo], axis=1), wcs, preferred_element_type=F32)
        ws[g] = jnp.exp(log_beta[:, sl] + cs[:, :LANE] + run)
        run = run + cs[:, LANE:]
    w = ws[0] if groups == 1 else jnp.concatenate(ws, axis=1)
    if valid is not None:
        w = jnp.where(valid, w, 0.0)
    return w, run


def _softmax_update(s, m_prev, l_prev):
    m_new = jnp.maximum(m_prev, jnp.max(s, axis=1, keepdims=True))
    alpha = jnp.exp(m_prev - m_new)
    p = jnp.exp(s - _lanes(m_new, s.shape[1]))
    return p, alpha, m_new, alpha * l_prev + jnp.sum(p, axis=1, keepdims=True)


def _df_lambda(lq1_ref, lk1_ref, lq2_ref, lk2_ref, lam_init):
    s1 = jnp.sum(lq1_ref[...] * lk1_ref[...], axis=1, keepdims=True)
    s2 = jnp.sum(lq2_ref[...] * lk2_ref[...], axis=1, keepdims=True)
    return jnp.exp(s1) - jnp.exp(s2) + lam_init


def _split_halves(q):
    first = lax.broadcasted_iota(jnp.int32, q.shape, 1) < LANE // 2
    return jnp.concatenate([jnp.where(first, q, 0.0), jnp.where(first, 0.0, q)], axis=0)


def _cast_kv(i, k_ref, v_ref, kb_ref, vb_ref):
    @pl.when(i == 0)
    def _():
        kb_ref[...] = k_ref[...].astype(BF16)
        vb_ref[...] = v_ref[...].astype(BF16)


def _sb_prompt_body(q_ref, k_ref, v_ref, g_ref, wcs_ref, o_ref, kb_ref, vb_ref, acc_ref, run_ref,
                    *, hb, tq, scale):
    i = pl.program_id(2)
    _cast_kv(i, k_ref, v_ref, kb_ref, vb_ref)
    qb = [q_ref[h].astype(BF16) for h in range(hb)]
    wcs = wcs_ref[...]
    acc_ref[...] = jnp.zeros_like(acc_ref)
    run_ref[...] = jnp.zeros_like(run_ref)

    def tile(j, valid):
        ks = pl.multiple_of(j * tq, tq)
        live = None
        for h in range(hb):
            kt = kb_ref[h, pl.ds(ks, tq), :]
            vt = vb_ref[h, pl.ds(ks, tq), :]
            z = lax.dot_general(qb[h], kt, _NT, preferred_element_type=F32) * scale
            w, run = _sb_weights(z, valid, run_ref[h], wcs)
            run_ref[h] = run
            acc_ref[h] += jnp.dot(w.astype(BF16), vt, preferred_element_type=F32)
            top = jnp.max(run)
            live = top if live is None else jnp.maximum(live, top)
        return live

    row = lax.broadcasted_iota(jnp.int32, (tq, tq), 0)
    col = lax.broadcasted_iota(jnp.int32, (tq, tq), 1)
    live = tile(i, col < row)

    def more(c):
        return (c[0] < i) & (c[1] > EXP_UNDERFLOW)

    def body(c):
        return c[0] + 1, tile(i - 1 - c[0], None)

    lax.while_loop(more, body, (jnp.int32(0), live))
    g = g_ref[...]
    for h in range(hb):
        o_ref[:, h * LANE:(h + 1) * LANE] = _rms(acc_ref[h], g).astype(o_ref.dtype)


def _df_prompt_body(q_ref, k_ref, v_ref, g_ref, lq1_ref, lk1_ref, lq2_ref, lk2_ref, o_ref,
                    kb_ref, vb_ref, m_ref, l_ref, acc_ref, *, hb, tq, scale, lam_init):
    i = pl.program_id(2)
    _cast_kv(i, k_ref, v_ref, kb_ref, vb_ref)
    qq = [_split_halves(q_ref[h] * scale).astype(BF16) for h in range(hb)]
    m_ref[...] = jnp.full_like(m_ref, MASK_NEG)
    l_ref[...] = jnp.zeros_like(l_ref)
    acc_ref[...] = jnp.zeros_like(acc_ref)

    def tile(j, valid):
        ks = pl.multiple_of(j * tq, tq)
        for h in range(hb):
            kt = kb_ref[h, pl.ds(ks, tq), :]
            vt = vb_ref[h, pl.ds(ks, tq), :]
            s = lax.dot_general(qq[h], kt, _NT, preferred_element_type=F32)
            if valid is not None:
                s = jnp.where(valid, s, MASK_NEG)
            p, alpha, m_ref[h], l_ref[h] = _softmax_update(s, m_ref[h], l_ref[h])
            acc_ref[h] = acc_ref[h] * alpha + jnp.dot(p.astype(BF16), vt, preferred_element_type=F32)

    row = lax.broadcasted_iota(jnp.int32, (2 * tq, tq), 0)
    col = lax.broadcasted_iota(jnp.int32, (2 * tq, tq), 1)
    row = jnp.where(row >= tq, row - tq, row)
    tile(i, col <= row)

    def body(jj, c):
        tile(jj, None)
        return c

    lax.fori_loop(0, i, body, 0)
    g = g_ref[...]
    lam = _df_lambda(lq1_ref, lk1_ref, lq2_ref, lk2_ref, lam_init)
    for h in range(hb):
        o = acc_ref[h] / l_ref[h]
        o = o[:tq] - lam * o[tq:]
        o_ref[:, h * LANE:(h + 1) * LANE] = (_rms(o, g) * (1.0 - lam_init)).astype(o_ref.dtype)


def _prompt_attention(kind, proj, kv_k, kv_v, layer, col0, norm_g, extra, tq, hb):
    batch, _, seq, _ = proj.shape
    heads = kv_k.shape[2]
    tq = _pick(seq, tq, LANE)
    nq = seq // tq
    hb = _heads_per_step(heads, col0, hb)
    qspec = pl.BlockSpec((None, hb, tq, LANE), lambda b, h, i: (b, col0 // hb + h, i, 0))
    kvspec = pl.BlockSpec((None, None, hb, seq, LANE), lambda b, h, i: (layer, b, h, 0, 0))
    const = lambda shape: pl.BlockSpec(shape, lambda b, h, i: (0,) * len(shape))
    kv_scratch = [pltpu.VMEM((hb, seq, LANE), BF16), pltpu.VMEM((hb, seq, LANE), BF16)]
    vmem = hb * (4 * _nbytes((seq, LANE), F32) + 4 * _nbytes((tq, LANE), F32) + 2 * _nbytes((seq, LANE), BF16))
    vmem += hb * 16 * _nbytes((2 * tq, tq), F32)
    if kind == "sb":
        body = functools.partial(_sb_prompt_body, hb=hb, tq=tq, scale=LANE ** -0.5)
        ins = [proj, kv_k, kv_v, norm_g, _cumsum_weights()]
        specs = [qspec, kvspec, kvspec, const((1, LANE)), const((2 * LANE, 2 * LANE))]
        scratch = kv_scratch + [pltpu.VMEM((hb, tq, LANE), F32)] * 2
    else:
        lam_init, lq1, lk1, lq2, lk2 = extra
        scale = (LANE // 2) ** -0.5
        assert math.frexp(scale)[0] == 0.5
        body = functools.partial(_df_prompt_body, hb=hb, tq=tq, scale=scale, lam_init=lam_init)
        ins = [proj, kv_k, kv_v, norm_g, lq1, lk1, lq2, lk2]
        specs = [qspec, kvspec, kvspec, const((1, LANE))] + [const((1, LANE // 2))] * 4
        scratch = kv_scratch + [pltpu.VMEM((hb, 2 * tq, LANE), F32)] * 3
    return pl.pallas_call(
        body,
        grid=(batch, heads // hb, nq),
        in_specs=specs,
        out_specs=pl.BlockSpec((tq, hb * LANE), lambda b, h, i: (b * nq + i, h)),
        out_shape=jax.ShapeDtypeStruct((batch * seq, heads * LANE), BF16),
        scratch_shapes=scratch,
        compiler_params=_params(3, vmem),
        name=kind + "_attention_prompt",
    )(*ins)


def _hgrn_body(q_ref, f_ref, i_ref, gate_ref, lbp_ref, gn_ref, lmat_ref, s0_ref, o_ref, s_ref,
               st_ref, qa_ref, key_ref, bl_ref, oraw_ref, *, hb, rows, layer, depth, t_valid, nt):
    t = pl.program_id(2)

    @pl.when(t == 0)
    def _():
        for h in range(hb):
            st_ref[h] = s0_ref[h].T

    lmat = lmat_ref[...]
    for h in range(hb):
        lbp = lbp_ref[:, h * LANE:(h + 1) * LANE]
        prow = [lbp[r:r + 1] for r in range(depth)]
        mx = functools.reduce(jnp.maximum, prow)
        es = [jnp.exp(r - mx) for r in prow]
        lb = sum(es[1:layer + 1], jnp.zeros_like(mx)) / sum(es[1:], es[0])
        p = f_ref[h]
        log_f = -_softplus(-p) + jnp.log1p(lb * jnp.exp(-p))
        key = (1.0 - lb) * (1.0 / (1.0 + jnp.exp(p)))
        if t_valid is not None:
            ok = (t * rows + lax.broadcasted_iota(jnp.int32, p.shape, 0)) < t_valid
            log_f = jnp.where(ok, log_f, 0.0)
            key = jnp.where(ok, key, 0.0)
        hi = log_f.astype(BF16)
        r1 = log_f - hi.astype(F32)
        mid = r1.astype(BF16)
        lo = (r1 - mid.astype(F32)).astype(BF16)
        bl_ref[h] = jnp.dot(lmat, jnp.concatenate([hi, mid, lo], axis=0), preferred_element_type=F32)
        qv = q_ref[h]
        qa_ref[h] = qv * _sigmoid(qv)
        key_ref[h] = key

    rowi = lax.broadcasted_iota(jnp.int32, (HG_CHUNK, LANE), 0)

    def chunk(c, carry):
        r0 = pl.multiple_of(c * HG_CHUNK, HG_CHUNK)
        for h in range(hb):
            blc = bl_ref[h, pl.ds(r0, HG_CHUNK), :]
            qac = qa_ref[h, pl.ds(r0, HG_CHUNK), :]
            kc = key_ref[h, pl.ds(r0, HG_CHUNK), :]
            vc = i_ref[h, pl.ds(r0, HG_CHUNK), :]
            blast = blc[HG_CHUNK - 1:HG_CHUNK]
            st = st_ref[h]
            o = lax.dot_general((qac * jnp.exp(blc)).astype(BF16), st.astype(BF16), _NT,
                                preferred_element_type=F32)
            for s in range(HG_CHUNK):
                e = jnp.exp(jnp.where(rowi >= s, blc - blc[s:s + 1], MASK_NEG))
                cv = jnp.sum(qac * e * kc[s:s + 1], axis=1, keepdims=True)
                o = o + cv * vc[s:s + 1]
            oraw_ref[h, pl.ds(r0, HG_CHUNK), :] = o
            kt = kc * jnp.exp(blast - blc)
            upd = lax.dot_general(vc.astype(BF16), kt.astype(BF16), _TN, preferred_element_type=F32)
            st_ref[h] = st * jnp.exp(blast) + upd
        return carry

    lax.fori_loop(0, rows // HG_CHUNK, chunk, 0, unroll=min(4, rows // HG_CHUNK))
    gn = gn_ref[...]
    for h in range(hb):
        gate = gate_ref[h]
        o_ref[:, h * LANE:(h + 1) * LANE] = (_rms(oraw_ref[h], gn) * (gate * _sigmoid(gate))).astype(o_ref.dtype)

    @pl.when(t == nt - 1)
    def _():
        for h in range(hb):
            s_ref[h] = st_ref[h].T


def _hgrn(proj, heads, col0, hg_lb, norm_g, s0, layer, t_valid, hb):
    batch, _, seq, _ = proj.shape
    rows = _pick(seq, 256, HG_CHUNK)
    nt = seq // rows
    depth = hg_lb.shape[0]
    hb = _heads_per_step(heads, col0, hb)
    r = jnp.arange(rows)
    lmat = ((r[:, None] // HG_CHUNK == r[None, :] // HG_CHUNK) & (r[None, :] <= r[:, None])).astype(BF16)
    lmat = jnp.concatenate([lmat] * 3, axis=1)

    def col(k):
        return pl.BlockSpec((None, hb, rows, LANE), lambda b, h, t: (b, (col0 + k * heads) // hb + h, t, 0))

    state = pl.BlockSpec((None, hb, LANE, LANE), lambda b, h, t: (b, h, 0, 0))
    vmem = hb * (2 * (5 * _nbytes((rows, LANE), F32) + 4 * _nbytes((LANE, LANE), F32)) + 16 * _nbytes((rows, LANE), F32))
    vmem += 2 * _nbytes((rows, 3 * rows), BF16)
    return pl.pallas_call(
        functools.partial(_hgrn_body, hb=hb, rows=rows, layer=layer, depth=depth,
                          t_valid=None if t_valid == seq else t_valid, nt=nt),
        grid=(batch, heads // hb, nt),
        in_specs=[col(0), col(1), col(2), col(3),
                  pl.BlockSpec((depth, hb * LANE), lambda b, h, t: (0, h)),
                  pl.BlockSpec((1, LANE), lambda b, h, t: (0, 0)),
                  pl.BlockSpec((rows, 3 * rows), lambda b, h, t: (0, 0)),
                  state],
        out_specs=[pl.BlockSpec((rows, hb * LANE), lambda b, h, t: (b * nt + t, h)), state],
        out_shape=[jax.ShapeDtypeStruct((batch * seq, heads * LANE), BF16),
                   jax.ShapeDtypeStruct(s0.shape, F32)],
        scratch_shapes=[pltpu.VMEM((hb, LANE, LANE), F32)] + [pltpu.VMEM((hb, rows, LANE), F32)] * 4,
        compiler_params=_params(3, vmem),
        name="hgrn2",
    )(proj, proj, proj, proj, hg_lb, norm_g, lmat, s0)


def _head_scores(qb, k_refs, heads, rows_per_head):
    out = []
    for h in range(heads):
        kh = [r[h].astype(BF16) for r in k_refs]
        kh = kh[0] if len(kh) == 1 else jnp.concatenate(kh, axis=0)
        out.append(lax.dot_general(qb[h * rows_per_head:(h + 1) * rows_per_head], kh, _NT,
                                   preferred_element_type=F32))
    return jnp.concatenate(out, axis=0)


def _head_values(w, v_refs, heads, rows_per_head):
    out = []
    for h in range(heads):
        vh = [r[h].astype(BF16) for r in v_refs]
        vh = vh[0] if len(vh) == 1 else jnp.concatenate(vh, axis=0)
        out.append(jnp.dot(w[h * rows_per_head:(h + 1) * rows_per_head].astype(BF16), vh,
                           preferred_element_type=F32))
    return jnp.concatenate(out, axis=0)


def _write_heads(o_ref, head_fn, heads):
    pad = jnp.zeros((o_ref.shape[0] - Q_SLOTS, LANE), F32)
    for h in range(heads):
        o_ref[:, h * LANE:(h + 1) * LANE] = jnp.concatenate([head_fn(h), pad], axis=0).astype(o_ref.dtype)


def _sb_decode_head_body(pt_ref, q_ref, kn_ref, vn_ref, *rest, group, heads, scale):
    k_refs, v_refs = rest[:group], rest[group:2 * group]
    wcs_ref, acc_ref, run_ref, live_ref = rest[2 * group:]
    p = pl.program_id(1)
    qb = q_ref[...]
    wcs = wcs_ref[...]

    @pl.when(p == 0)
    def _():
        z = _head_scores(qb, [kn_ref], heads, Q_SLOTS) * scale
        tok = lax.broadcasted_iota(jnp.int32, z.shape, 0) & (Q_SLOTS - 1)
        col = lax.broadcasted_iota(jnp.int32, z.shape, 1)
        w, run = _sb_weights(z, col < tok, jnp.zeros(run_ref.shape, F32), wcs)
        run_ref[...] = run
        acc_ref[...] = _head_values(w, [vn_ref], heads, Q_SLOTS)

    @pl.when(p == 1)
    def _():
        z = _head_scores(qb, k_refs, heads, Q_SLOTS) * scale
        w, run = _sb_weights(z, None, run_ref[...], wcs)
        run_ref[...] = run
        acc_ref[...] += _head_values(w, v_refs, heads, Q_SLOTS)
        live_ref[...] = jnp.full(live_ref.shape, 1, jnp.int32) * (jnp.max(run) > EXP_UNDERFLOW).astype(jnp.int32)


def _sb_decode_tail_body(pt_ref, live_ref, q_ref, *rest, group, heads, scale):
    k_refs, v_refs = rest[:group], rest[group:2 * group]
    g_ref, wcs_ref, acc0_ref, run0_ref, o_ref, acc_ref, run_ref = rest[2 * group:]
    b, s = pl.program_id(0), pl.program_id(1)

    @pl.when(s == 0)
    def _():
        acc_ref[...] = acc0_ref[...]
        run_ref[...] = run0_ref[...]

    @pl.when(live_ref[b] > 0)
    def _():
        z = _head_scores(q_ref[...], k_refs, heads, Q_SLOTS) * scale
        w, run = _sb_weights(z, None, run_ref[...], wcs_ref[...])
        run_ref[...] = run
        acc_ref[...] += _head_values(w, v_refs, heads, Q_SLOTS)

    @pl.when(s == pl.num_programs(1) - 1)
    def _():
        g = g_ref[...]
        _write_heads(o_ref, lambda h: _rms(acc_ref[h * Q_SLOTS:(h + 1) * Q_SLOTS, :], g), heads)


def _df_decode_body(pt_ref, q_ref, kn_ref, vn_ref, *rest, group, heads, scale, lam_init):
    k_refs, v_refs = rest[:group], rest[group:2 * group]
    g_ref, lq1_ref, lk1_ref, lq2_ref, lk2_ref, o_ref, m_ref, l_ref, acc_ref = rest[2 * group:]
    p = pl.program_id(1)
    qb = q_ref[...]
    rph = 2 * Q_SLOTS

    def step(s, v_parts, first):
        m_prev = jnp.full(m_ref.shape, MASK_NEG, F32) if first else m_ref[...]
        l_prev = jnp.zeros(l_ref.shape, F32) if first else l_ref[...]
        pr, alpha, m_ref[...], l_ref[...] = _softmax_update(s, m_prev, l_prev)
        pv = _head_values(pr, v_parts, heads, rph)
        acc_ref[...] = pv if first else acc_ref[...] * alpha + pv

    @pl.when(p == 0)
    def _():
        s = _head_scores(qb, [kn_ref], heads, rph) * scale
        tok = lax.broadcasted_iota(jnp.int32, s.shape, 0) & (Q_SLOTS - 1)
        col = lax.broadcasted_iota(jnp.int32, s.shape, 1)
        step(jnp.where(col <= tok, s, MASK_NEG), [vn_ref], True)

    @pl.when(p > 0)
    def _():
        step(_head_scores(qb, k_refs, heads, rph) * scale, v_refs, False)

    @pl.when(p == pl.num_programs(1) - 1)
    def _():
        g = g_ref[...]
        lam = _df_lambda(lq1_ref, lk1_ref, lq2_ref, lk2_ref, lam_init)

        def head(h):
            r1 = slice(h * rph, h * rph + Q_SLOTS)
            r2 = slice(h * rph + Q_SLOTS, (h + 1) * rph)
            o = acc_ref[r1, :] / l_ref[r1, :] - lam * (acc_ref[r2, :] / l_ref[r2, :])
            return _rms(o, g) * (1.0 - lam_init)

        _write_heads(o_ref, head, heads)


def _decode_inputs(q, k_new, v_new, page, split):
    B, heads, tq, _ = q.shape
    pad = ((0, 0), (0, 0), (0, page - tq), (0, 0))
    k_new, v_new = jnp.pad(k_new, pad), jnp.pad(v_new, pad)
    q = jnp.pad(q, ((0, 0), (0, 0), (0, Q_SLOTS - tq), (0, 0)))
    if split:
        first = jnp.arange(LANE) < LANE // 2
        q = jnp.stack([jnp.where(first, q, 0.0), jnp.where(first, 0.0, q)], axis=2)
    return q.reshape(B, -1, LANE).astype(BF16), k_new, v_new


def _page_specs(layer, heads, page, group, page_of):
    def spec(gi):
        return pl.BlockSpec((None, None, heads, page, LANE),
                            lambda b, s, *pre: (layer, page_of(gi, b, s, *pre), 0, 0, 0))
    return [spec(gi) for gi in range(group)]


def _decode_vmem(heads, page, group, qrows):
    vmem = 2 * (2 * group + 2) * _nbytes((heads, page, LANE), F32)
    return vmem + 4 * _nbytes((heads, group * page, LANE), BF16) + 24 * _nbytes((qrows, group * page), F32)


def _sb_decode_attention(q, k_new, v_new, pool_k, pool_v, page_table, layer, norm_g, out_rows, group):
    B, heads, _, _ = q.shape
    page = pool_k.shape[3]
    n_pages = page_table.shape[1]
    group = _pick(n_pages // 2, group, 1)
    n_groups = n_pages // group
    qb, k_new, v_new = _decode_inputs(q, k_new, v_new, page, False)
    qrows = heads * Q_SLOTS
    wcs = _cumsum_weights()
    scale = LANE ** -0.5
    vmem = _decode_vmem(heads, page, group, qrows)
    per_b = lambda shape: pl.BlockSpec((None,) + shape, lambda b, s, *pre: (b,) + (0,) * len(shape))
    const = lambda shape: pl.BlockSpec(shape, lambda b, s, *pre: (0,) * len(shape))
    state = jax.ShapeDtypeStruct((B, qrows, LANE), F32)

    last = _page_specs(layer, heads, page, group, lambda gi, b, s, pt: pt[b, (n_groups - 1) * group + gi])
    acc, run, live = pl.pallas_call(
        functools.partial(_sb_decode_head_body, group=group, heads=heads, scale=scale),
        grid_spec=pltpu.PrefetchScalarGridSpec(
            num_scalar_prefetch=1,
            grid=(B, 2),
            in_specs=[per_b((qrows, LANE)), per_b((heads, page, LANE)), per_b((heads, page, LANE))]
            + last + last + [const((2 * LANE, 2 * LANE))],
            out_specs=[per_b((qrows, LANE)), per_b((qrows, LANE)), per_b((SUBLANE, LANE))],
        ),
        out_shape=[state, state, jax.ShapeDtypeStruct((B, SUBLANE, LANE), jnp.int32)],
        compiler_params=_params(2, vmem),
        name="sb_attention_decode_head",
    )(page_table, qb, k_new, v_new, *([pool_k] * group), *([pool_v] * group), wcs)
    live = live[:, 0, 0]

    def earlier(gi, b, s, pt, lv):
        grp = jnp.where(lv[b] > 0, n_groups - 2 - s, 0)
        return pt[b, grp * group + gi]

    pages = _page_specs(layer, heads, page, group, earlier)
    return pl.pallas_call(
        functools.partial(_sb_decode_tail_body, group=group, heads=heads, scale=scale),
        grid_spec=pltpu.PrefetchScalarGridSpec(
            num_scalar_prefetch=2,
            grid=(B, n_groups - 1),
            in_specs=[per_b((qrows, LANE))] + pages + pages
            + [const((1, LANE)), const((2 * LANE, 2 * LANE)), per_b((qrows, LANE)), per_b((qrows, LANE))],
            out_specs=per_b((out_rows, heads * LANE)),
            scratch_shapes=[pltpu.VMEM((qrows, LANE), F32), pltpu.VMEM((qrows, LANE), F32)],
        ),
        out_shape=jax.ShapeDtypeStruct((B, out_rows, heads * LANE), BF16),
        compiler_params=_params(2, vmem),
        name="sb_attention_decode_tail",
    )(page_table, live, qb, *([pool_k] * group), *([pool_v] * group), norm_g, wcs, acc, run)


def _df_decode_attention(q, k_new, v_new, pool_k, pool_v, page_table, layer, norm_g, extra, out_rows, group):
    B, heads, _, _ = q.shape
    page = pool_k.shape[3]
    n_pages = page_table.shape[1]
    group = _pick(n_pages, group, 1)
    n_groups = n_pages // group
    qb, k_new, v_new = _decode_inputs(q, k_new, v_new, page, True)
    qrows = heads * 2 * Q_SLOTS
    lam_init, lq1, lk1, lq2, lk2 = extra
    per_b = lambda shape: pl.BlockSpec((None,) + shape, lambda b, p, pt: (b,) + (0,) * len(shape))
    const = lambda shape: pl.BlockSpec(shape, lambda b, p, pt: (0,) * len(shape))
    pages = _page_specs(layer, heads, page, group,
                        lambda gi, b, p, pt: pt[b, (n_groups - jnp.maximum(p, 1)) * group + gi])
    return pl.pallas_call(
        functools.partial(_df_decode_body, group=group, heads=heads, scale=(LANE // 2) ** -0.5,
                          lam_init=lam_init),
        grid_spec=pltpu.PrefetchScalarGridSpec(
            num_scalar_prefetch=1,
            grid=(B, n_groups + 1),
            in_specs=[per_b((qrows, LANE)), per_b((heads, page, LANE)), per_b((heads, page, LANE))]
            + pages + pages + [const((1, LANE))] + [const((1, LANE // 2))] * 4,
            out_specs=per_b((out_rows, heads * LANE)),
            scratch_shapes=[pltpu.VMEM((qrows, LANE), F32)] * 3,
        ),
        out_shape=jax.ShapeDtypeStruct((B, out_rows, heads * LANE), BF16),
        compiler_params=_params(2, _decode_vmem(heads, page, group, qrows)),
        name="df_attention_decode",
    )(page_table, qb, k_new, v_new, *([pool_k] * group), *([pool_v] * group), norm_g, lq1, lk1, lq2, lk2)


def _mixer_half(x, xb, batch, seq, t_valid, layer, w, kv_prev, s_hg_past, cache, page_table, dims):
    h_sb, h_hg, h_df = dims
    lam_init = 0.8 - 0.6 * math.exp(-0.3 * layer)
    c_hg, c_df = 3 * h_sb, 3 * h_sb + 4 * h_hg
    kv_cols = ((h_sb, h_sb), (2 * h_sb, h_sb), (c_df + h_df, h_df), (c_df + 2 * h_df, h_df))
    proj, kv = _in_projection(xb, w["in"], layer, batch, seq, kv_cols, kv_prev)
    row = lambda name: w[name][layer][None, :]
    df_extra = (lam_init, row("lq1"), row("lk1"), row("lq2"), row("lk2"))
    if cache is None:
        o_sb = _prompt_attention("sb", proj, kv[0], kv[1], layer, 0, row("sb_g"), None, 256, 4)
        o_df = _prompt_attention("df", proj, kv[2], kv[3], layer, c_df, row("df_g"), df_extra, 256, 4)
    else:
        new = lambda a: a[layer, :, :, :t_valid]
        o_sb = _sb_decode_attention(proj[:, :h_sb, :t_valid], new(kv[0]), new(kv[1]), cache[0], cache[1],
                                    page_table, layer, row("sb_g"), seq, 8).reshape(batch * seq, -1)
        o_df = _df_decode_attention(proj[:, c_df:c_df + h_df, :t_valid], new(kv[2]), new(kv[3]), cache[2],
                                    cache[3], page_table, layer, row("df_g"), df_extra, seq, 8
                                    ).reshape(batch * seq, -1)
    o_hg, s_hg = _hgrn(proj, h_hg, c_hg, w["hg_lb"], row("hg_g"), s_hg_past, layer, t_valid, 4)
    m = _out_projection([o_sb, o_hg, o_df], w["out"], layer, 2048, 512)
    h, hb = _residual_layernorm(x, m, row("ln1_g"), row("ln1_b"), _alpha(w), "layernorm_mixer")
    return h, hb, kv, s_hg


def _alpha(w):
    return (2 * w["in"].shape[0]) ** 0.25


def _ffn_half(h, hid, layer, w):
    f = _matmul(hid, w["down"], layer, 1024, 256, "swiglu_down")
    return _residual_layernorm(h, f, w["ln2_g"][layer][None, :], w["ln2_b"][layer][None, :], _alpha(w),
                               "layernorm_ffn")


def kernel(x_prompt, x_sample, cache_sb_k, cache_sb_v, cache_df_k, cache_df_v, state_hg, page_table, w_in, w_out, sb_norm_g, hg_lb, hg_norm_g, df_lq1, df_lk1, df_lq2, df_lk2, df_norm_g, ln1_g, ln1_b, w_gate, w_up, w_down, ln2_g, ln2_b):
    depth = w_in.shape[0]
    bp, seq, d_model = x_prompt.shape
    bs, dec_seq, _ = x_sample.shape
    dims = (cache_sb_k.shape[3], state_hg.shape[2], cache_df_k.shape[3])
    assert dec_seq <= Q_SLOTS and cache_sb_k.shape[4] == LANE and state_hg.shape[3:] == (LANE, LANE)
    dec_rows = HG_CHUNK
    w = {"in": w_in, "out": w_out, "gate": w_gate, "up": w_up,
         "down": w_down.astype(BF16),
         "sb_g": sb_norm_g, "hg_g": hg_norm_g, "df_g": df_norm_g, "hg_lb": hg_lb,
         "lq1": df_lq1, "lk1": df_lk1, "lq2": df_lq2, "lk2": df_lk2,
         "ln1_g": ln1_g, "ln1_b": ln1_b, "ln2_g": ln2_g, "ln2_b": ln2_b}

    xp = x_prompt.reshape(bp * seq, d_model)
    xs = jnp.pad(x_sample, ((0, 0), (0, dec_rows - dec_seq), (0, 0))).reshape(bs * dec_rows, d_model)
    xpb, xsb = xp.astype(BF16), xs.astype(BF16)
    cache = tuple(jnp.transpose(c, (0, 1, 3, 2, 4)) for c in (cache_sb_k, cache_sb_v, cache_df_k, cache_df_v))
    s_zero = jnp.zeros((bp,) + state_hg.shape[2:], state_hg.dtype)
    kv_p, kv_s, st_p, st_s = (), (), [], []
    for layer in range(depth):
        hp, hpb, kv_p, s = _mixer_half(xp, xpb, bp, seq, seq, layer, w, kv_p, s_zero, None, None, dims)
        st_p.append(s)
        hs, hsb, kv_s, s = _mixer_half(xs, xsb, bs, dec_rows, dec_seq, layer, w, kv_s, state_hg[layer], cache,
                                       page_table, dims)
        st_s.append(s)
        hid_p, hid_s = _gateup(hpb, hsb, w["gate"], w["up"], layer, 2048, 256)
        xp, xpb = _ffn_half(hp, hid_p, layer, w)
        xs, xsb = _ffn_half(hs, hid_s, layer, w)
    y_prompt = xp.reshape(bp, seq, d_model)
    y_sample = xs.reshape(bs, dec_rows, d_model)[:, :dec_seq]
    tokens_major = lambda a, t: jnp.transpose(a[:, :, :, :t], (0, 1, 3, 2, 4))
    return (y_prompt, y_sample,
            *(tokens_major(a, seq) for a in kv_p), jnp.stack(st_p),
            *(tokens_major(a, dec_seq) for a in kv_s), jnp.stack(st_s))
```

```python
import functools
import math

import jax
import jax.numpy as jnp
from jax import lax
from jax.experimental import pallas as pl
from jax.experimental.pallas import tpu as pltpu

F32 = jnp.float32
BF16 = jnp.bfloat16

LANE = 128
SUBLANE = 8
VMEM_CAP = 62 * 1024 * 1024
HG_CHUNK = 16
Q_SLOTS = SUBLANE
LN_EPS = 1e-5
RMS_EPS = 1e-6
MASK_NEG = -1e30
EXP_UNDERFLOW = -104.0

_NT = (((1,), (1,)), ((), ()))
_TN = (((0,), (0,)), ((), ()))


def _params(n_axes, vmem_bytes):
    limit = int(min(VMEM_CAP, max(32 * 1024 * 1024, vmem_bytes)))
    return pltpu.CompilerParams(dimension_semantics=("arbitrary",) * n_axes, vmem_limit_bytes=limit)


def _pick(dim, pref, align):
    if dim <= pref:
        return dim
    t = (pref // align) * align
    while t >= align:
        if dim % t == 0:
            return t
        t -= align
    return dim


def _heads_per_step(heads, col0, pref):
    hb = pref
    while heads % hb or col0 % hb:
        hb -= 1
    return hb


def _nbytes(shape, dtype):
    return math.prod(shape) * jnp.dtype(dtype).itemsize


def _resident(shape, index):
    return pl.BlockSpec(shape, index, pipeline_mode=pl.Buffered(1))


def _mm_body(x_ref, w_ref, o_ref):
    o_ref[...] = jnp.dot(x_ref[...], w_ref[...].astype(BF16), preferred_element_type=F32)


def _matmul(x, w, layer, tm, tn, name):
    M, K = x.shape
    N = w.shape[2]
    tm, tn = _pick(M, tm, SUBLANE), _pick(N, tn, LANE)
    vmem = _nbytes((tm, K), BF16) + 2 * (_nbytes((K, tn), w.dtype) + _nbytes((tm, tn), F32))
    vmem += 3 * _nbytes((tm, tn), F32) + _nbytes((K, tn), BF16)
    return pl.pallas_call(
        _mm_body,
        grid=(M // tm, N // tn),
        in_specs=[_resident((tm, K), lambda i, j: (i, 0)),
                  pl.BlockSpec((None, K, tn), lambda i, j: (layer, 0, j))],
        out_specs=pl.BlockSpec((tm, tn), lambda i, j: (i, j)),
        out_shape=jax.ShapeDtypeStruct((M, N), F32),
        compiler_params=_params(2, vmem),
        name=name,
    )(x, w)


def _out_proj_body(*refs, widths):
    n = len(widths)
    x_refs, xs_refs = refs[:n], refs[n:2 * n]
    w_ref, o_ref, os_ref = refs[2 * n:]
    starts = [sum(widths[:k]) for k in range(n)]
    wb = [w_ref[r0:r0 + kw, :].astype(BF16) for r0, kw in zip(starts, widths)]

    def project(parts):
        acc = None
        for x_ref, wk in zip(parts, wb):
            part = jnp.dot(x_ref[...], wk, preferred_element_type=F32)
            acc = part if acc is None else acc + part
        return acc

    o_ref[...] = project(x_refs)

    @pl.when(pl.program_id(0) == 0)
    def _():
        os_ref[...] = project(xs_refs)


def _out_projection(xs, xs_small, w, layer, tm, tn):
    M, Ms = xs[0].shape[0], xs_small[0].shape[0]
    widths = tuple(x.shape[1] for x in xs)
    K, N = w.shape[1:]
    assert sum(widths) == K
    tm, tn = _pick(M, tm, SUBLANE), _pick(N, tn, LANE)
    nj = N // tn
    vmem = _nbytes((tm, K), BF16) + 2 * (_nbytes((K, tn), F32) + _nbytes((tm, tn), F32))
    vmem += 3 * _nbytes((tm, tn), F32) + _nbytes((K, tn), BF16) + 4 * _nbytes((Ms, K), BF16)
    return pl.pallas_call(
        functools.partial(_out_proj_body, widths=widths),
        grid=(M // tm, nj),
        in_specs=[_resident((tm, kw), lambda i, j: (i, 0)) for kw in widths]
        + [_resident((Ms, kw), lambda i, j: (0, 0)) for kw in widths]
        + [pl.BlockSpec((None, K, tn), lambda i, j: (layer, 0, j))],
        out_specs=[pl.BlockSpec((tm, tn), lambda i, j: (i, j)),
                   pl.BlockSpec((Ms, tn), lambda i, j: (0, jnp.where(i == 0, j, nj - 1)))],
        out_shape=[jax.ShapeDtypeStruct((M, N), F32), jax.ShapeDtypeStruct((Ms, N), F32)],
        compiler_params=_params(2, vmem),
        name="out_projection",
    )(*xs, *xs_small, w)


def _in_proj_body(x_ref, xs_ref, w_ref, *refs, kv_tiles):
    n = len(kv_tiles)
    outs = refs[len(refs) - 2 * (1 + n):]
    o_ref, kv_refs, os_ref, kvs_refs = outs[0], outs[1:1 + n], outs[1 + n], outs[2 + n:]
    wb = w_ref[...].astype(BF16)
    j = pl.program_id(1)

    def project(src_ref, dst_ref, kv_dst):
        res = jnp.dot(src_ref[...], wb, preferred_element_type=F32)
        nb, cols, rows, _ = dst_ref.shape
        for b in range(nb):
            for c in range(cols):
                dst_ref[b, c] = res[b * rows:(b + 1) * rows, c * LANE:(c + 1) * LANE]
        for (j0, nn), ref in zip(kv_tiles, kv_dst):

            @pl.when((j >= j0) & (j < j0 + nn))
            def _(ref=ref):
                ref[...] = dst_ref[...]

    project(x_ref, o_ref, kv_refs)

    @pl.when(pl.program_id(0) == 0)
    def _():
        project(xs_ref, os_ref, kvs_refs)


def _in_projection(x, xs, w, layer, shape, shape_small, kv_cols, kv_prev, kv_prev_small):
    (batch, seq), (batch_s, seq_s) = shape, shape_small
    M, K = x.shape
    Ms = xs.shape[0]
    depth, _, N = w.shape
    cols = N // LANE
    kc = 2
    while cols % kc or any(c0 % kc or n % kc for c0, n in kv_cols):
        kc //= 2
    tn = kc * LANE
    nj = cols // kc
    if seq >= 1024:
        nb, rows = 1, _pick(seq, 2048, SUBLANE)
    else:
        nb, rows = _pick(batch, max(1, 1024 // seq), 1), seq
    tm = nb * rows
    per_seq = seq // rows
    where = (lambda i: (i // per_seq, i % per_seq)) if nb == 1 else (lambda i: (i, 0))
    kv_tiles = tuple((c0 // kc, n // kc) for c0, n in kv_cols)

    def kv_spec(j0, n):
        def index(i, j):
            b, r = where(i)
            return (layer, b, jnp.clip(j - j0, 0, n - 1), r, 0)
        return pl.BlockSpec((None, nb, kc, rows, LANE), index)

    def proj_index(i, j):
        b, r = where(i)
        return (b, j, r, 0)

    def kv_spec_small(j0, n):
        return pl.BlockSpec((None, batch_s, kc, seq_s, LANE),
                            lambda i, j: (layer, 0, jnp.where(i == 0, jnp.clip(j - j0, 0, n - 1), n - 1), 0, 0))

    proj_small = pl.BlockSpec((batch_s, kc, seq_s, LANE), lambda i, j: (0, jnp.where(i == 0, j, nj - 1), 0, 0))

    n_kv, n_prev, n_prev_s = len(kv_cols), len(kv_prev), len(kv_prev_small)
    vmem = _nbytes((tm, K), BF16) + 2 * _nbytes((K, tn), F32) + _nbytes((K, tn), BF16)
    vmem += (2 * (1 + n_kv) + 5) * _nbytes((tm, tn), F32)
    vmem += 2 * _nbytes((Ms, K), BF16) + (2 * (1 + n_kv) + 3) * _nbytes((Ms, tn), F32)
    aliases = {3 + a: 1 + a for a in range(n_prev)}
    aliases.update({3 + n_prev + a: 2 + n_kv + a for a in range(n_prev_s)})
    outs = pl.pallas_call(
        functools.partial(_in_proj_body, kv_tiles=kv_tiles),
        grid=(M // tm, nj),
        in_specs=[_resident((tm, K), lambda i, j: (i, 0)),
                  _resident((Ms, K), lambda i, j: (0, 0)),
                  pl.BlockSpec((None, K, tn), lambda i, j: (layer, 0, j))]
        + [pl.BlockSpec(memory_space=pl.ANY)] * (n_prev + n_prev_s),
        out_specs=[pl.BlockSpec((nb, kc, rows, LANE), proj_index)] + [kv_spec(j0, n) for j0, n in kv_tiles]
        + [proj_small] + [kv_spec_small(j0, n) for j0, n in kv_tiles],
        out_shape=[jax.ShapeDtypeStruct((batch, cols, seq, LANE), F32)]
        + [jax.ShapeDtypeStruct((depth, batch, n, seq, LANE), F32) for _, n in kv_cols]
        + [jax.ShapeDtypeStruct((batch_s, cols, seq_s, LANE), F32)]
        + [jax.ShapeDtypeStruct((depth, batch_s, n, seq_s, LANE), F32) for _, n in kv_cols],
        input_output_aliases=aliases,
        compiler_params=_params(2, vmem),
        name="in_projection",
    )(x, xs, w, *kv_prev, *kv_prev_small)
    return outs[0], tuple(outs[1:1 + n_kv]), outs[1 + n_kv], tuple(outs[2 + n_kv:])


def _sigmoid(x):
    return 1.0 / (1.0 + jnp.exp(-x))


def _gateup_body(x_ref, xs_ref, wg_ref, wu_ref, o_ref, os_ref):
    wg, wu = wg_ref[...].astype(BF16), wu_ref[...].astype(BF16)

    def swiglu(x):
        g = jnp.dot(x, wg, preferred_element_type=F32)
        u = jnp.dot(x, wu, preferred_element_type=F32)
        return (g * _sigmoid(g) * u).astype(BF16)

    o_ref[...] = swiglu(x_ref[...])

    @pl.when(pl.program_id(0) == 0)
    def _():
        os_ref[...] = swiglu(xs_ref[...])


def _gateup(x, xs, wg, wu, layer, tm, tn):
    M, K = x.shape
    Ms = xs.shape[0]
    N = wg.shape[2]
    tm, tn = _pick(M, tm, SUBLANE), _pick(N, tn, LANE)
    nj = N // tn
    vmem = _nbytes((tm, K), BF16) + 2 * (2 * _nbytes((K, tn), F32) + _nbytes((tm, tn), BF16))
    vmem += 4 * _nbytes((tm, tn), F32) + 2 * _nbytes((K, tn), BF16) + 4 * _nbytes((Ms, K), BF16)
    wspec = pl.BlockSpec((None, K, tn), lambda i, j: (layer, 0, j))
    small_out = pl.BlockSpec((Ms, tn), lambda i, j: (0, jnp.where(i == 0, j, nj - 1)))
    return pl.pallas_call(
        _gateup_body,
        grid=(M // tm, nj),
        in_specs=[_resident((tm, K), lambda i, j: (i, 0)), _resident((Ms, K), lambda i, j: (0, 0)),
                  wspec, wspec],
        out_specs=[pl.BlockSpec((tm, tn), lambda i, j: (i, j)), small_out],
        out_shape=[jax.ShapeDtypeStruct((M, N), BF16), jax.ShapeDtypeStruct((Ms, N), BF16)],
        compiler_params=_params(2, vmem),
        name="swiglu_gate_up",
    )(x, xs, wg, wu)


def _ln_body(x_ref, m_ref, g_ref, b_ref, o_ref, ob_ref, *, alpha):
    v = alpha * x_ref[...] + m_ref[...]
    mu = jnp.mean(v, axis=-1, keepdims=True)
    d = v - mu
    var = jnp.mean(d * d, axis=-1, keepdims=True)
    y = d * lax.rsqrt(var + LN_EPS) * g_ref[...] + b_ref[...]
    o_ref[...] = y
    ob_ref[...] = y.astype(BF16)


def _residual_layernorm(x, m, g, b, alpha, name):
    M, D = x.shape
    tm = _pick(M, 256, SUBLANE)
    vmem = 2 * (3 * _nbytes((tm, D), F32) + _nbytes((tm, D), BF16)) + 4 * _nbytes((tm, D), F32)
    row = pl.BlockSpec((tm, D), lambda i: (i, 0))
    vec = pl.BlockSpec((1, D), lambda i: (0, 0))
    return pl.pallas_call(
        functools.partial(_ln_body, alpha=alpha),
        grid=(M // tm,),
        in_specs=[row, row, vec, vec],
        out_specs=[row, row],
        out_shape=[jax.ShapeDtypeStruct((M, D), F32), jax.ShapeDtypeStruct((M, D), BF16)],
        compiler_params=_params(1, vmem),
        name=name,
    )(x, m, g, b)


def _softplus(z):
    return jnp.maximum(z, 0.0) + jnp.log(1.0 + jnp.exp(-jnp.abs(z)))


def _rms(o, g):
    return o * lax.rsqrt(jnp.mean(o * o, axis=-1, keepdims=True) + RMS_EPS) * g


def _lanes(x, n):
    reps = n // LANE
    return x if reps == 1 else jnp.concatenate([x] * reps, axis=1)


def _cumsum_weights():
    j = jnp.arange(LANE)[:, None]
    s = jnp.arange(LANE)[None, :]
    blk = jnp.concatenate([(j > s).astype(BF16), jnp.ones((LANE, LANE), BF16)], axis=1)
    return jnp.concatenate([blk, blk], axis=0)


def _sb_weights(z, valid, run, wcs):
    sp = _softplus(z)
    log_keep = -sp if valid is None else jnp.where(valid, -sp, 0.0)
    log_beta = z - sp
    groups = z.shape[1] // LANE
    ws = [None] * groups
    for g in reversed(range(groups)):
        sl = slice(g * LANE, (g + 1) * LANE)
        lk = log_keep[:, sl]
        hi = lk.astype(BF16)
        lo = (lk - hi.astype(F32)).astype(BF16)
        cs = jnp.dot(jnp.concatenate([hi, lo], axis=1), wcs, preferred_element_type=F32)
        ws[g] = jnp.exp(log_beta[:, sl] + cs[:, :LANE] + run)
        run = run + cs[:, LANE:]
    w = ws[0] if groups == 1 else jnp.concatenate(ws, axis=1)
    if valid is not None:
        w = jnp.where(valid, w, 0.0)
    return w, run


def _softmax_update(s, m_prev, l_prev):
    m_new = jnp.maximum(m_prev, jnp.max(s, axis=1, keepdims=True))
    alpha = jnp.exp(m_prev - m_new)
    p = jnp.exp(s - _lanes(m_new, s.shape[1]))
    return p, alpha, m_new, alpha * l_prev + jnp.sum(p, axis=1, keepdims=True)


def _df_lambda(lq1_ref, lk1_ref, lq2_ref, lk2_ref, lam_init):
    s1 = jnp.sum(lq1_ref[...] * lk1_ref[...], axis=1, keepdims=True)
    s2 = jnp.sum(lq2_ref[...] * lk2_ref[...], axis=1, keepdims=True)
    return jnp.exp(s1) - jnp.exp(s2) + lam_init


def _split_halves(q):
    first = lax.broadcasted_iota(jnp.int32, q.shape, 1) < LANE // 2
    return jnp.concatenate([jnp.where(first, q, 0.0), jnp.where(first, 0.0, q)], axis=0)


def _cast_kv(i, k_ref, v_ref, kb_ref, vb_ref):
    @pl.when(i == 0)
    def _():
        kb_ref[...] = k_ref[...].astype(BF16)
        vb_ref[...] = v_ref[...].astype(BF16)


def _sb_prompt_body(q_ref, k_ref, v_ref, g_ref, wcs_ref, o_ref, kb_ref, vb_ref, acc_ref, run_ref,
                    *, hb, tq, scale):
    i = pl.program_id(2)
    _cast_kv(i, k_ref, v_ref, kb_ref, vb_ref)
    qb = [q_ref[h].astype(BF16) for h in range(hb)]
    wcs = wcs_ref[...]
    acc_ref[...] = jnp.zeros_like(acc_ref)
    run_ref[...] = jnp.zeros_like(run_ref)

    def tile(j, valid):
        ks = pl.multiple_of(j * tq, tq)
        live = None
        for h in range(hb):
            kt = kb_ref[h, pl.ds(ks, tq), :]
            vt = vb_ref[h, pl.ds(ks, tq), :]
            z = lax.dot_general(qb[h], kt, _NT, preferred_element_type=F32) * scale
            w, run = _sb_weights(z, valid, run_ref[h], wcs)
            run_ref[h] = run
            acc_ref[h] += jnp.dot(w.astype(BF16), vt, preferred_element_type=F32)
            top = jnp.max(run)
            live = top if live is None else jnp.maximum(live, top)
        return live

    row = lax.broadcasted_iota(jnp.int32, (tq, tq), 0)
    col = lax.broadcasted_iota(jnp.int32, (tq, tq), 1)
    live = tile(i, col < row)

    def more(c):
        return (c[0] < i) & (c[1] > EXP_UNDERFLOW)

    def body(c):
        return c[0] + 1, tile(i - 1 - c[0], None)

    lax.while_loop(more, body, (jnp.int32(0), live))
    g = g_ref[...]
    for h in range(hb):
        o_ref[:, h * LANE:(h + 1) * LANE] = _rms(acc_ref[h], g).astype(o_ref.dtype)


def _df_prompt_body(q_ref, k_ref, v_ref, g_ref, lq1_ref, lk1_ref, lq2_ref, lk2_ref, o_ref,
                    kb_ref, vb_ref, m_ref, l_ref, acc_ref, *, hb, tq, scale, lam_init):
    i = pl.program_id(2)
    _cast_kv(i, k_ref, v_ref, kb_ref, vb_ref)
    qq = [_split_halves(q_ref[h] * scale).astype(BF16) for h in range(hb)]
    m_ref[...] = jnp.full_like(m_ref, MASK_NEG)
    l_ref[...] = jnp.zeros_like(l_ref)
    acc_ref[...] = jnp.zeros_like(acc_ref)

    def tile(j, valid):
        ks = pl.multiple_of(j * tq, tq)
        for h in range(hb):
            kt = kb_ref[h, pl.ds(ks, tq), :]
            vt = vb_ref[h, pl.ds(ks, tq), :]
            s = lax.dot_general(qq[h], kt, _NT, preferred_element_type=F32)
            if valid is not None:
                s = jnp.where(valid, s, MASK_NEG)
            p, alpha, m_ref[h], l_ref[h] = _softmax_update(s, m_ref[h], l_ref[h])
            acc_ref[h] = acc_ref[h] * alpha + jnp.dot(p.astype(BF16), vt, preferred_element_type=F32)

    row = lax.broadcasted_iota(jnp.int32, (2 * tq, tq), 0)
    col = lax.broadcasted_iota(jnp.int32, (2 * tq, tq), 1)
    row = jnp.where(row >= tq, row - tq, row)
    tile(i, col <= row)

    def body(jj, c):
        tile(jj, None)
        return c

    lax.fori_loop(0, i, body, 0)
    g = g_ref[...]
    lam = _df_lambda(lq1_ref, lk1_ref, lq2_ref, lk2_ref, lam_init)
    for h in range(hb):
        o = acc_ref[h] / l_ref[h]
        o = o[:tq] - lam * o[tq:]
        o_ref[:, h * LANE:(h + 1) * LANE] = (_rms(o, g) * (1.0 - lam_init)).astype(o_ref.dtype)


def _prompt_attention(kind, proj, kv_k, kv_v, layer, col0, norm_g, extra, tq, hb):
    batch, _, seq, _ = proj.shape
    heads = kv_k.shape[2]
    tq = _pick(seq, tq, LANE)
    nq = seq // tq
    hb = _heads_per_step(heads, col0, hb)
    qspec = pl.BlockSpec((None, hb, tq, LANE), lambda b, h, i: (b, col0 // hb + h, i, 0))
    kvspec = pl.BlockSpec((None, None, hb, seq, LANE), lambda b, h, i: (layer, b, h, 0, 0))
    const = lambda shape: pl.BlockSpec(shape, lambda b, h, i: (0,) * len(shape))
    kv_scratch = [pltpu.VMEM((hb, seq, LANE), BF16), pltpu.VMEM((hb, seq, LANE), BF16)]
    vmem = hb * (4 * _nbytes((seq, LANE), F32) + 4 * _nbytes((tq, LANE), F32) + 2 * _nbytes((seq, LANE), BF16))
    vmem += hb * 16 * _nbytes((2 * tq, tq), F32)
    if kind == "sb":
        body = functools.partial(_sb_prompt_body, hb=hb, tq=tq, scale=LANE ** -0.5)
        ins = [proj, kv_k, kv_v, norm_g, _cumsum_weights()]
        specs = [qspec, kvspec, kvspec, const((1, LANE)), const((2 * LANE, 2 * LANE))]
        scratch = kv_scratch + [pltpu.VMEM((hb, tq, LANE), F32)] * 2
    else:
        lam_init, lq1, lk1, lq2, lk2 = extra
        scale = (LANE // 2) ** -0.5
        assert math.frexp(scale)[0] == 0.5
        body = functools.partial(_df_prompt_body, hb=hb, tq=tq, scale=scale, lam_init=lam_init)
        ins = [proj, kv_k, kv_v, norm_g, lq1, lk1, lq2, lk2]
        specs = [qspec, kvspec, kvspec, const((1, LANE))] + [const((1, LANE // 2))] * 4
        scratch = kv_scratch + [pltpu.VMEM((hb, 2 * tq, LANE), F32)] * 3
    return pl.pallas_call(
        body,
        grid=(batch, heads // hb, nq),
        in_specs=specs,
        out_specs=pl.BlockSpec((tq, hb * LANE), lambda b, h, i: (b * nq + i, h)),
        out_shape=jax.ShapeDtypeStruct((batch * seq, heads * LANE), BF16),
        scratch_shapes=scratch,
        compiler_params=_params(3, vmem),
        name=kind + "_attention_prompt",
    )(*ins)


def _hgrn_body(q_ref, f_ref, i_ref, gate_ref, lbp_ref, gn_ref, lmat_ref, s0_ref, o_ref, s_ref,
               st_ref, qa_ref, key_ref, bl_ref, oraw_ref, *, hb, rows, layer, depth, t_valid, nt):
    t = pl.program_id(2)

    @pl.when(t == 0)
    def _():
        for h in range(hb):
            st_ref[h] = s0_ref[h].T

    lmat = lmat_ref[...]
    for h in range(hb):
        lbp = lbp_ref[:, h * LANE:(h + 1) * LANE]
        prow = [lbp[r:r + 1] for r in range(depth)]
        mx = functools.reduce(jnp.maximum, prow)
        es = [jnp.exp(r - mx) for r in prow]
        lb = sum(es[1:layer + 1], jnp.zeros_like(mx)) / sum(es[1:], es[0])
        p = f_ref[h]
        log_f = -_softplus(-p) + jnp.log1p(lb * jnp.exp(-p))
        key = (1.0 - lb) * (1.0 / (1.0 + jnp.exp(p)))
        if t_valid is not None:
            ok = (t * rows + lax.broadcasted_iota(jnp.int32, p.shape, 0)) < t_valid
            log_f = jnp.where(ok, log_f, 0.0)
            key = jnp.where(ok, key, 0.0)
        hi = log_f.astype(BF16)
        r1 = log_f - hi.astype(F32)
        mid = r1.astype(BF16)
        lo = (r1 - mid.astype(F32)).astype(BF16)
        bl_ref[h] = jnp.dot(lmat, jnp.concatenate([hi, mid, lo], axis=0), preferred_element_type=F32)
        qv = q_ref[h]
        qa_ref[h] = qv * _sigmoid(qv)
        key_ref[h] = key

    rowi = lax.broadcasted_iota(jnp.int32, (HG_CHUNK, LANE), 0)

    def chunk(c, carry):
        r0 = pl.multiple_of(c * HG_CHUNK, HG_CHUNK)
        for h in range(hb):
            blc = bl_ref[h, pl.ds(r0, HG_CHUNK), :]
            qac = qa_ref[h, pl.ds(r0, HG_CHUNK), :]
            kc = key_ref[h, pl.ds(r0, HG_CHUNK), :]
            vc = i_ref[h, pl.ds(r0, HG_CHUNK), :]
            blast = blc[HG_CHUNK - 1:HG_CHUNK]
            st = st_ref[h]
            o = lax.dot_general((qac * jnp.exp(blc)).astype(BF16), st.astype(BF16), _NT,
                                preferred_element_type=F32)
            for s in range(HG_CHUNK):
                e = jnp.exp(jnp.where(rowi >= s, blc - blc[s:s + 1], MASK_NEG))
                cv = jnp.sum(qac * e * kc[s:s + 1], axis=1, keepdims=True)
                o = o + cv * vc[s:s + 1]
            oraw_ref[h, pl.ds(r0, HG_CHUNK), :] = o
            kt = kc * jnp.exp(blast - blc)
            upd = lax.dot_general(vc.astype(BF16), kt.astype(BF16), _TN, preferred_element_type=F32)
            st_ref[h] = st * jnp.exp(blast) + upd
        return carry

    lax.fori_loop(0, rows // HG_CHUNK, chunk, 0, unroll=min(4, rows // HG_CHUNK))
    gn = gn_ref[...]
    for h in range(hb):
        gate = gate_ref[h]
        o_ref[:, h * LANE:(h + 1) * LANE] = (_rms(oraw_ref[h], gn) * (gate * _sigmoid(gate))).astype(o_ref.dtype)

    @pl.when(t == nt - 1)
    def _():
        for h in range(hb):
            s_ref[h] = st_ref[h].T


def _hgrn(proj, heads, col0, hg_lb, norm_g, s0, layer, t_valid, hb):
    batch, _, seq, _ = proj.shape
    rows = _pick(seq, 256, HG_CHUNK)
    nt = seq // rows
    depth = hg_lb.shape[0]
    hb = _heads_per_step(heads, col0, hb)
    r = jnp.arange(rows)
    lmat = ((r[:, None] // HG_CHUNK == r[None, :] // HG_CHUNK) & (r[None, :] <= r[:, None])).astype(BF16)
    lmat = jnp.concatenate([lmat] * 3, axis=1)

    def col(k):
        return pl.BlockSpec((None, hb, rows, LANE), lambda b, h, t: (b, (col0 + k * heads) // hb + h, t, 0))

    state = pl.BlockSpec((None, hb, LANE, LANE), lambda b, h, t: (b, h, 0, 0))
    vmem = hb * (2 * (5 * _nbytes((rows, LANE), F32) + 4 * _nbytes((LANE, LANE), F32)) + 16 * _nbytes((rows, LANE), F32))
    vmem += 2 * _nbytes((rows, 3 * rows), BF16)
    return pl.pallas_call(
        functools.partial(_hgrn_body, hb=hb, rows=rows, layer=layer, depth=depth,
                          t_valid=None if t_valid == seq else t_valid, nt=nt),
        grid=(batch, heads // hb, nt),
        in_specs=[col(0), col(1), col(2), col(3),
                  pl.BlockSpec((depth, hb * LANE), lambda b, h, t: (0, h)),
                  pl.BlockSpec((1, LANE), lambda b, h, t: (0, 0)),
                  pl.BlockSpec((rows, 3 * rows), lambda b, h, t: (0, 0)),
                  state],
        out_specs=[pl.BlockSpec((rows, hb * LANE), lambda b, h, t: (b * nt + t, h)), state],
        out_shape=[jax.ShapeDtypeStruct((batch * seq, heads * LANE), BF16),
                   jax.ShapeDtypeStruct(s0.shape, F32)],
        scratch_shapes=[pltpu.VMEM((hb, LANE, LANE), F32)] + [pltpu.VMEM((hb, rows, LANE), F32)] * 4,
        compiler_params=_params(3, vmem),
        name="hgrn2",
    )(proj, proj, proj, proj, hg_lb, norm_g, lmat, s0)


def _head_scores(qb, k_refs, heads, rows_per_head):
    out = []
    for h in range(heads):
        kh = [r[h].astype(BF16) for r in k_refs]
        kh = kh[0] if len(kh) == 1 else jnp.concatenate(kh, axis=0)
        out.append(lax.dot_general(qb[h * rows_per_head:(h + 1) * rows_per_head], kh, _NT,
                                   preferred_element_type=F32))
    return jnp.concatenate(out, axis=0)


def _head_values(w, v_refs, heads, rows_per_head):
    out = []
    for h in range(heads):
        vh = [r[h].astype(BF16) for r in v_refs]
        vh = vh[0] if len(vh) == 1 else jnp.concatenate(vh, axis=0)
        out.append(jnp.dot(w[h * rows_per_head:(h + 1) * rows_per_head].astype(BF16), vh,
                           preferred_element_type=F32))
    return jnp.concatenate(out, axis=0)


def _write_heads(o_ref, head_fn, heads):
    pad = jnp.zeros((o_ref.shape[0] - Q_SLOTS, LANE), F32)
    for h in range(heads):
        o_ref[:, h * LANE:(h + 1) * LANE] = jnp.concatenate([head_fn(h), pad], axis=0).astype(o_ref.dtype)


def _sb_decode_head_body(pt_ref, q_ref, kn_ref, vn_ref, *rest, group, heads, scale):
    k_refs, v_refs = rest[:group], rest[group:2 * group]
    wcs_ref, acc_ref, run_ref, live_ref = rest[2 * group:]
    p = pl.program_id(1)
    qb = q_ref[...]
    wcs = wcs_ref[...]

    @pl.when(p == 0)
    def _():
        z = _head_scores(qb, [kn_ref], heads, Q_SLOTS) * scale
        tok = lax.broadcasted_iota(jnp.int32, z.shape, 0) & (Q_SLOTS - 1)
        col = lax.broadcasted_iota(jnp.int32, z.shape, 1)
        w, run = _sb_weights(z, col < tok, jnp.zeros(run_ref.shape, F32), wcs)
        run_ref[...] = run
        acc_ref[...] = _head_values(w, [vn_ref], heads, Q_SLOTS)

    @pl.when(p == 1)
    def _():
        z = _head_scores(qb, k_refs, heads, Q_SLOTS) * scale
        w, run = _sb_weights(z, None, run_ref[...], wcs)
        run_ref[...] = run
        acc_ref[...] += _head_values(w, v_refs, heads, Q_SLOTS)
        live_ref[...] = jnp.full(live_ref.shape, 1, jnp.int32) * (jnp.max(run) > EXP_UNDERFLOW).astype(jnp.int32)


def _sb_decode_finish_body(acc_ref, g_ref, o_ref, *, heads):
    g = g_ref[...]
    _write_heads(o_ref, lambda h: _rms(acc_ref[h * Q_SLOTS:(h + 1) * Q_SLOTS, :], g), heads)


def _sb_decode_tail_body(pt_ref, live_ref, q_ref, *rest, group, heads, scale):
    k_refs, v_refs = rest[:group], rest[group:2 * group]
    g_ref, wcs_ref, acc0_ref, run0_ref, o_ref, acc_ref, run_ref = rest[2 * group:]
    b, s = pl.program_id(0), pl.program_id(1)

    @pl.when(s == 0)
    def _():
        acc_ref[...] = acc0_ref[...]
        run_ref[...] = run0_ref[...]

    @pl.when(live_ref[b] > 0)
    def _():
        z = _head_scores(q_ref[...], k_refs, heads, Q_SLOTS) * scale
        w, run = _sb_weights(z, None, run_ref[...], wcs_ref[...])
        run_ref[...] = run
        acc_ref[...] += _head_values(w, v_refs, heads, Q_SLOTS)

    @pl.when(s == pl.num_programs(1) - 1)
    def _():
        g = g_ref[...]
        _write_heads(o_ref, lambda h: _rms(acc_ref[h * Q_SLOTS:(h + 1) * Q_SLOTS, :], g), heads)


def _df_decode_body(pt_ref, q_ref, kn_ref, vn_ref, *rest, group, heads, scale, lam_init):
    k_refs, v_refs = rest[:group], rest[group:2 * group]
    g_ref, lq1_ref, lk1_ref, lq2_ref, lk2_ref, o_ref, m_ref, l_ref, acc_ref = rest[2 * group:]
    p = pl.program_id(1)
    qb = q_ref[...]
    rph = 2 * Q_SLOTS

    def step(s, v_parts, first):
        m_prev = jnp.full(m_ref.shape, MASK_NEG, F32) if first else m_ref[...]
        l_prev = jnp.zeros(l_ref.shape, F32) if first else l_ref[...]
        pr, alpha, m_ref[...], l_ref[...] = _softmax_update(s, m_prev, l_prev)
        pv = _head_values(pr, v_parts, heads, rph)
        acc_ref[...] = pv if first else acc_ref[...] * alpha + pv

    @pl.when(p == 0)
    def _():
        s = _head_scores(qb, [kn_ref], heads, rph) * scale
        tok = lax.broadcasted_iota(jnp.int32, s.shape, 0) & (Q_SLOTS - 1)
        col = lax.broadcasted_iota(jnp.int32, s.shape, 1)
        step(jnp.where(col <= tok, s, MASK_NEG), [vn_ref], True)

    @pl.when(p > 0)
    def _():
        step(_head_scores(qb, k_refs, heads, rph) * scale, v_refs, False)

    @pl.when(p == pl.num_programs(1) - 1)
    def _():
        g = g_ref[...]
        lam = _df_lambda(lq1_ref, lk1_ref, lq2_ref, lk2_ref, lam_init)

        def head(h):
            r1 = slice(h * rph, h * rph + Q_SLOTS)
            r2 = slice(h * rph + Q_SLOTS, (h + 1) * rph)
            o = acc_ref[r1, :] / l_ref[r1, :] - lam * (acc_ref[r2, :] / l_ref[r2, :])
            return _rms(o, g) * (1.0 - lam_init)

        _write_heads(o_ref, head, heads)


def _decode_inputs(q, k_new, v_new, page, split):
    B, heads, tq, _ = q.shape
    pad = ((0, 0), (0, 0), (0, page - tq), (0, 0))
    k_new, v_new = jnp.pad(k_new, pad), jnp.pad(v_new, pad)
    q = jnp.pad(q, ((0, 0), (0, 0), (0, Q_SLOTS - tq), (0, 0)))
    if split:
        first = jnp.arange(LANE) < LANE // 2
        q = jnp.stack([jnp.where(first, q, 0.0), jnp.where(first, 0.0, q)], axis=2)
    return q.reshape(B, -1, LANE).astype(BF16), k_new, v_new


def _page_specs(layer, heads, page, group, page_of):
    def spec(gi):
        return pl.BlockSpec((None, None, heads, page, LANE),
                            lambda b, s, *pre: (layer, page_of(gi, b, s, *pre), 0, 0, 0))
    return [spec(gi) for gi in range(group)]


def _decode_vmem(heads, page, group, qrows):
    vmem = 2 * (2 * group + 2) * _nbytes((heads, page, LANE), F32)
    return vmem + 4 * _nbytes((heads, group * page, LANE), BF16) + 24 * _nbytes((qrows, group * page), F32)


def _sb_decode_attention(q, k_new, v_new, pool_k, pool_v, page_table, layer, norm_g, out_rows, group):
    B, heads, _, _ = q.shape
    page = pool_k.shape[3]
    n_pages = page_table.shape[1]
    group = _pick(n_pages // 2, group, 1)
    n_groups = n_pages // group
    qb, k_new, v_new = _decode_inputs(q, k_new, v_new, page, False)
    qrows = heads * Q_SLOTS
    wcs = _cumsum_weights()
    scale = LANE ** -0.5
    vmem = _decode_vmem(heads, page, group, qrows)
    per_b = lambda shape: pl.BlockSpec((None,) + shape, lambda b, s, *pre: (b,) + (0,) * len(shape))
    const = lambda shape: pl.BlockSpec(shape, lambda b, s, *pre: (0,) * len(shape))
    state = jax.ShapeDtypeStruct((B, qrows, LANE), F32)

    last = _page_specs(layer, heads, page, group, lambda gi, b, s, pt: pt[b, (n_groups - 1) * group + gi])
    acc, run, live = pl.pallas_call(
        functools.partial(_sb_decode_head_body, group=group, heads=heads, scale=scale),
        grid_spec=pltpu.PrefetchScalarGridSpec(
            num_scalar_prefetch=1,
            grid=(B, 2),
            in_specs=[per_b((qrows, LANE)), per_b((heads, page, LANE)), per_b((heads, page, LANE))]
            + last + last + [const((2 * LANE, 2 * LANE))],
            out_specs=[per_b((qrows, LANE)), per_b((qrows, LANE)), per_b((SUBLANE, LANE))],
        ),
        out_shape=[state, state, jax.ShapeDtypeStruct((B, SUBLANE, LANE), jnp.int32)],
        compiler_params=_params(2, vmem),
        name="sb_attention_decode_head",
    )(page_table, qb, k_new, v_new, *([pool_k] * group), *([pool_v] * group), wcs)
    live = live[:, 0, 0]

    def earlier(gi, b, s, pt, lv):
        grp = jnp.where(lv[b] > 0, n_groups - 2 - s, 0)
        return pt[b, grp * group + gi]

    pages = _page_specs(layer, heads, page, group, earlier)
    out_shape = jax.ShapeDtypeStruct((B, out_rows, heads * LANE), BF16)

    def rest():
        return pl.pallas_call(
            functools.partial(_sb_decode_tail_body, group=group, heads=heads, scale=scale),
            grid_spec=pltpu.PrefetchScalarGridSpec(
                num_scalar_prefetch=2,
                grid=(B, n_groups - 1),
                in_specs=[per_b((qrows, LANE))] + pages + pages
                + [const((1, LANE)), const((2 * LANE, 2 * LANE)), per_b((qrows, LANE)), per_b((qrows, LANE))],
                out_specs=per_b((out_rows, heads * LANE)),
                scratch_shapes=[pltpu.VMEM((qrows, LANE), F32), pltpu.VMEM((qrows, LANE), F32)],
            ),
            out_shape=out_shape,
            compiler_params=_params(2, vmem),
            name="sb_attention_decode_tail",
        )(page_table, live, qb, *([pool_k] * group), *([pool_v] * group), norm_g, wcs, acc, run)

    def finish():
        return pl.pallas_call(
            functools.partial(_sb_decode_finish_body, heads=heads),
            grid=(B,),
            in_specs=[pl.BlockSpec((None, qrows, LANE), lambda b: (b, 0, 0)), pl.BlockSpec((1, LANE), lambda b: (0, 0))],
            out_specs=pl.BlockSpec((None, out_rows, heads * LANE), lambda b: (b, 0, 0)),
            out_shape=out_shape,
            compiler_params=_params(1, 0),
            name="sb_attention_decode_finish",
        )(acc, norm_g)

    return lax.cond(jnp.any(live > 0), rest, finish)


def _df_decode_attention(q, k_new, v_new, pool_k, pool_v, page_table, layer, norm_g, extra, out_rows, group):
    B, heads, _, _ = q.shape
    page = pool_k.shape[3]
    n_pages = page_table.shape[1]
    group = _pick(n_pages, group, 1)
    n_groups = n_pages // group
    qb, k_new, v_new = _decode_inputs(q, k_new, v_new, page, True)
    qrows = heads * 2 * Q_SLOTS
    lam_init, lq1, lk1, lq2, lk2 = extra
    per_b = lambda shape: pl.BlockSpec((None,) + shape, lambda b, p, pt: (b,) + (0,) * len(shape))
    const = lambda shape: pl.BlockSpec(shape, lambda b, p, pt: (0,) * len(shape))
    pages = _page_specs(layer, heads, page, group,
                        lambda gi, b, p, pt: pt[b, (n_groups - jnp.maximum(p, 1)) * group + gi])
    return pl.pallas_call(
        functools.partial(_df_decode_body, group=group, heads=heads, scale=(LANE // 2) ** -0.5,
                          lam_init=lam_init),
        grid_spec=pltpu.PrefetchScalarGridSpec(
            num_scalar_prefetch=1,
            grid=(B, n_groups + 1),
            in_specs=[per_b((qrows, LANE)), per_b((heads, page, LANE)), per_b((heads, page, LANE))]
            + pages + pages + [const((1, LANE))] + [const((1, LANE // 2))] * 4,
            out_specs=per_b((out_rows, heads * LANE)),
            scratch_shapes=[pltpu.VMEM((qrows, LANE), F32)] * 3,
        ),
        out_shape=jax.ShapeDtypeStruct((B, out_rows, heads * LANE), BF16),
        compiler_params=_params(2, _decode_vmem(heads, page, group, qrows)),
        name="df_attention_decode",
    )(page_table, qb, k_new, v_new, *([pool_k] * group), *([pool_v] * group), norm_g, lq1, lk1, lq2, lk2)


def _mixers(proj, kv, batch, seq, t_valid, layer, w, s_hg_past, cache, page_table, dims):
    h_sb, h_hg, h_df = dims
    lam_init = 0.8 - 0.6 * math.exp(-0.3 * layer)
    c_hg, c_df = 3 * h_sb, 3 * h_sb + 4 * h_hg
    row = lambda name: w[name][layer][None, :]
    df_extra = (lam_init, row("lq1"), row("lk1"), row("lq2"), row("lk2"))
    if cache is None:
        o_sb = _prompt_attention("sb", proj, kv[0], kv[1], layer, 0, row("sb_g"), None, 256, 4)
        o_df = _prompt_attention("df", proj, kv[2], kv[3], layer, c_df, row("df_g"), df_extra, 256, 4)
    else:
        new = lambda a: a[layer, :, :, :t_valid]
        o_sb = _sb_decode_attention(proj[:, :h_sb, :t_valid], new(kv[0]), new(kv[1]), cache[0], cache[1],
                                    page_table, layer, row("sb_g"), seq, 8).reshape(batch * seq, -1)
        o_df = _df_decode_attention(proj[:, c_df:c_df + h_df, :t_valid], new(kv[2]), new(kv[3]), cache[2],
                                    cache[3], page_table, layer, row("df_g"), df_extra, seq, 8
                                    ).reshape(batch * seq, -1)
    o_hg, s_hg = _hgrn(proj, h_hg, c_hg, w["hg_lb"], row("hg_g"), s_hg_past, layer, t_valid, 4)
    return [o_sb, o_hg, o_df], s_hg


def kernel(x_prompt, x_sample, cache_sb_k, cache_sb_v, cache_df_k, cache_df_v, state_hg, page_table, w_in, w_out, sb_norm_g, hg_lb, hg_norm_g, df_lq1, df_lk1, df_lq2, df_lk2, df_norm_g, ln1_g, ln1_b, w_gate, w_up, w_down, ln2_g, ln2_b):
    depth = w_in.shape[0]
    bp, seq, d_model = x_prompt.shape
    bs, dec_seq, _ = x_sample.shape
    h_sb, h_hg, h_df = dims = (cache_sb_k.shape[3], state_hg.shape[2], cache_df_k.shape[3])
    assert dec_seq <= Q_SLOTS and cache_sb_k.shape[4] == LANE and state_hg.shape[3:] == (LANE, LANE)
    dec_rows = HG_CHUNK
    w = {"down": w_down.astype(BF16),
         "sb_g": sb_norm_g, "hg_g": hg_norm_g, "df_g": df_norm_g, "hg_lb": hg_lb,
         "lq1": df_lq1, "lk1": df_lk1, "lq2": df_lq2, "lk2": df_lk2}
    alpha = (2 * depth) ** 0.25
    c_df = 3 * h_sb + 4 * h_hg
    kv_cols = ((h_sb, h_sb), (2 * h_sb, h_sb), (c_df + h_df, h_df), (c_df + 2 * h_df, h_df))

    xp = x_prompt.reshape(bp * seq, d_model)
    xs = jnp.pad(x_sample, ((0, 0), (0, dec_rows - dec_seq), (0, 0))).reshape(bs * dec_rows, d_model)
    xpb, xsb = xp.astype(BF16), xs.astype(BF16)
    cache = tuple(jnp.transpose(c, (0, 1, 3, 2, 4)) for c in (cache_sb_k, cache_sb_v, cache_df_k, cache_df_v))
    s_zero = jnp.zeros((bp,) + state_hg.shape[2:], state_hg.dtype)
    kv_p, kv_s, st_p, st_s = (), (), [], []
    for layer in range(depth):
        vec = lambda a: a[layer][None, :]
        proj_p, kv_p, proj_s, kv_s = _in_projection(xpb, xsb, w_in, layer, (bp, seq), (bs, dec_rows), kv_cols,
                                                    kv_p, kv_s)
        mix_p, s = _mixers(proj_p, kv_p, bp, seq, seq, layer, w, s_zero, None, None, dims)
        st_p.append(s)
        mix_s, s = _mixers(proj_s, kv_s, bs, dec_rows, dec_seq, layer, w, state_hg[layer], cache, page_table,
                           dims)
        st_s.append(s)
        m_p, m_s = _out_projection(mix_p, mix_s, w_out, layer, 2048, 512)
        hp, hpb = _residual_layernorm(xp, m_p, vec(ln1_g), vec(ln1_b), alpha, "layernorm_mixer")
        hs, hsb = _residual_layernorm(xs, m_s, vec(ln1_g), vec(ln1_b), alpha, "layernorm_mixer")
        hid_p, hid_s = _gateup(hpb, hsb, w_gate, w_up, layer, 2048, 256)
        f_p = _matmul(hid_p, w["down"], layer, 1024, 256, "swiglu_down")
        f_s = _matmul(hid_s, w["down"], layer, 1024, 256, "swiglu_down")
        xp, xpb = _residual_layernorm(hp, f_p, vec(ln2_g), vec(ln2_b), alpha, "layernorm_ffn")
        xs, xsb = _residual_layernorm(hs, f_s, vec(ln2_g), vec(ln2_b), alpha, "layernorm_ffn")
    y_prompt = xp.reshape(bp, seq, d_model)
    y_sample = xs.reshape(bs, dec_rows, d_model)[:, :dec_seq]
    tokens_major = lambda a, t: jnp.transpose(a[:, :, :, :t], (0, 1, 3, 2, 4))
    return (y_prompt, y_sample,
            *(tokens_major(a, seq) for a in kv_p), jnp.stack(st_p),
            *(tokens_major(a, dec_seq) for a in kv_s), jnp.stack(st_s))
```

```python
import functools
import math

import jax
import jax.numpy as jnp
from jax import lax
from jax.experimental import pallas as pl
from jax.experimental.pallas import tpu as pltpu

F32 = jnp.float32
BF16 = jnp.bfloat16

LANE = 128
SUBLANE = 8
VMEM_CAP = 62 * 1024 * 1024
HG_CHUNK = 16
Q_SLOTS = SUBLANE
LN_EPS = 1e-5
RMS_EPS = 1e-6
MASK_NEG = -1e30
EXP_UNDERFLOW = -104.0

_NT = (((1,), (1,)), ((), ()))
_TN = (((0,), (0,)), ((), ()))


def _params(n_axes, vmem_bytes):
    limit = int(min(VMEM_CAP, max(32 * 1024 * 1024, vmem_bytes)))
    return pltpu.CompilerParams(dimension_semantics=("arbitrary",) * n_axes, vmem_limit_bytes=limit)


def _pick(dim, pref, align):
    if dim <= pref:
        return dim
    t = (pref // align) * align
    while t >= align:
        if dim % t == 0:
            return t
        t -= align
    return dim


def _heads_per_step(heads, col0, pref):
    hb = pref
    while heads % hb or col0 % hb:
        hb -= 1
    return hb


def _nbytes(shape, dtype):
    return math.prod(shape) * jnp.dtype(dtype).itemsize


def _resident(shape, index):
    return pl.BlockSpec(shape, index, pipeline_mode=pl.Buffered(1))


def _mm_body(x_ref, xs_ref, w_ref, o_ref, os_ref):
    wb = w_ref[...].astype(BF16)
    o_ref[...] = jnp.dot(x_ref[...], wb, preferred_element_type=F32)

    @pl.when(pl.program_id(0) == 0)
    def _():
        os_ref[...] = jnp.dot(xs_ref[...], wb, preferred_element_type=F32)


def _matmul(x, xs, w, layer, tm, tn, name):
    M, K = x.shape
    Ms = xs.shape[0]
    N = w.shape[2]
    tm, tn = _pick(M, tm, SUBLANE), _pick(N, tn, LANE)
    nj = N // tn
    vmem = _nbytes((tm, K), BF16) + 2 * (_nbytes((K, tn), w.dtype) + _nbytes((tm, tn), F32))
    vmem += 3 * _nbytes((tm, tn), F32) + _nbytes((K, tn), BF16) + 2 * _nbytes((Ms, K), BF16)
    return pl.pallas_call(
        _mm_body,
        grid=(M // tm, nj),
        in_specs=[_resident((tm, K), lambda i, j: (i, 0)), _resident((Ms, K), lambda i, j: (0, 0)),
                  pl.BlockSpec((None, K, tn), lambda i, j: (layer, 0, j))],
        out_specs=[pl.BlockSpec((tm, tn), lambda i, j: (i, j)),
                   pl.BlockSpec((Ms, tn), lambda i, j: (0, jnp.where(i == 0, j, nj - 1)))],
        out_shape=[jax.ShapeDtypeStruct((M, N), F32), jax.ShapeDtypeStruct((Ms, N), F32)],
        compiler_params=_params(2, vmem),
        name=name,
    )(x, xs, w)


def _out_proj_body(*refs, widths):
    n = len(widths)
    x_refs, xs_refs = refs[:n], refs[n:2 * n]
    w_ref, o_ref, os_ref = refs[2 * n:]
    starts = [sum(widths[:k]) for k in range(n)]
    wb = [w_ref[r0:r0 + kw, :].astype(BF16) for r0, kw in zip(starts, widths)]

    def project(parts):
        acc = None
        for x_ref, wk in zip(parts, wb):
            part = jnp.dot(x_ref[...], wk, preferred_element_type=F32)
            acc = part if acc is None else acc + part
        return acc

    o_ref[...] = project(x_refs)

    @pl.when(pl.program_id(0) == 0)
    def _():
        os_ref[...] = project(xs_refs)


def _out_projection(xs, xs_small, w, layer, tm, tn):
    M, Ms = xs[0].shape[0], xs_small[0].shape[0]
    widths = tuple(x.shape[1] for x in xs)
    K, N = w.shape[1:]
    assert sum(widths) == K
    tm, tn = _pick(M, tm, SUBLANE), _pick(N, tn, LANE)
    nj = N // tn
    vmem = _nbytes((tm, K), BF16) + 2 * (_nbytes((K, tn), F32) + _nbytes((tm, tn), F32))
    vmem += 3 * _nbytes((tm, tn), F32) + _nbytes((K, tn), BF16) + 4 * _nbytes((Ms, K), BF16)
    return pl.pallas_call(
        functools.partial(_out_proj_body, widths=widths),
        grid=(M // tm, nj),
        in_specs=[_resident((tm, kw), lambda i, j: (i, 0)) for kw in widths]
        + [_resident((Ms, kw), lambda i, j: (0, 0)) for kw in widths]
        + [pl.BlockSpec((None, K, tn), lambda i, j: (layer, 0, j))],
        out_specs=[pl.BlockSpec((tm, tn), lambda i, j: (i, j)),
                   pl.BlockSpec((Ms, tn), lambda i, j: (0, jnp.where(i == 0, j, nj - 1)))],
        out_shape=[jax.ShapeDtypeStruct((M, N), F32), jax.ShapeDtypeStruct((Ms, N), F32)],
        compiler_params=_params(2, vmem),
        name="out_projection",
    )(*xs, *xs_small, w)


def _in_proj_body(x_ref, xs_ref, w_ref, *refs, kv_tiles):
    n = len(kv_tiles)
    outs = refs[len(refs) - 2 * (1 + n):]
    o_ref, kv_refs, os_ref, kvs_refs = outs[0], outs[1:1 + n], outs[1 + n], outs[2 + n:]
    wb = w_ref[...].astype(BF16)
    j = pl.program_id(1)

    def project(src_ref, dst_ref, kv_dst):
        res = jnp.dot(src_ref[...], wb, preferred_element_type=F32)
        nb, cols, rows, _ = dst_ref.shape
        for b in range(nb):
            for c in range(cols):
                dst_ref[b, c] = res[b * rows:(b + 1) * rows, c * LANE:(c + 1) * LANE]
        for (j0, nn), ref in zip(kv_tiles, kv_dst):

            @pl.when((j >= j0) & (j < j0 + nn))
            def _(ref=ref):
                ref[...] = dst_ref[...]

    project(x_ref, o_ref, kv_refs)

    @pl.when(pl.program_id(0) == 0)
    def _():
        project(xs_ref, os_ref, kvs_refs)


def _in_projection(x, xs, w, layer, shape, shape_small, kv_cols, kv_prev, kv_prev_small):
    (batch, seq), (batch_s, seq_s) = shape, shape_small
    M, K = x.shape
    Ms = xs.shape[0]
    depth, _, N = w.shape
    cols = N // LANE
    kc = 2
    while cols % kc or any(c0 % kc or n % kc for c0, n in kv_cols):
        kc //= 2
    tn = kc * LANE
    nj = cols // kc
    if seq >= 1024:
        nb, rows = 1, _pick(seq, 2048, SUBLANE)
    else:
        nb, rows = _pick(batch, max(1, 1024 // seq), 1), seq
    tm = nb * rows
    per_seq = seq // rows
    where = (lambda i: (i // per_seq, i % per_seq)) if nb == 1 else (lambda i: (i, 0))
    kv_tiles = tuple((c0 // kc, n // kc) for c0, n in kv_cols)

    def kv_spec(j0, n):
        def index(i, j):
            b, r = where(i)
            return (layer, b, jnp.clip(j - j0, 0, n - 1), r, 0)
        return pl.BlockSpec((None, nb, kc, rows, LANE), index)

    def proj_index(i, j):
        b, r = where(i)
        return (b, j, r, 0)

    def kv_spec_small(j0, n):
        return pl.BlockSpec((None, batch_s, kc, seq_s, LANE),
                            lambda i, j: (layer, 0, jnp.where(i == 0, jnp.clip(j - j0, 0, n - 1), n - 1), 0, 0))

    proj_small = pl.BlockSpec((batch_s, kc, seq_s, LANE), lambda i, j: (0, jnp.where(i == 0, j, nj - 1), 0, 0))

    n_kv, n_prev, n_prev_s = len(kv_cols), len(kv_prev), len(kv_prev_small)
    vmem = _nbytes((tm, K), BF16) + 2 * _nbytes((K, tn), F32) + _nbytes((K, tn), BF16)
    vmem += (2 * (1 + n_kv) + 5) * _nbytes((tm, tn), F32)
    vmem += 2 * _nbytes((Ms, K), BF16) + (2 * (1 + n_kv) + 3) * _nbytes((Ms, tn), F32)
    aliases = {3 + a: 1 + a for a in range(n_prev)}
    aliases.update({3 + n_prev + a: 2 + n_kv + a for a in range(n_prev_s)})
    outs = pl.pallas_call(
        functools.partial(_in_proj_body, kv_tiles=kv_tiles),
        grid=(M // tm, nj),
        in_specs=[_resident((tm, K), lambda i, j: (i, 0)),
                  _resident((Ms, K), lambda i, j: (0, 0)),
                  pl.BlockSpec((None, K, tn), lambda i, j: (layer, 0, j))]
        + [pl.BlockSpec(memory_space=pl.ANY)] * (n_prev + n_prev_s),
        out_specs=[pl.BlockSpec((nb, kc, rows, LANE), proj_index)] + [kv_spec(j0, n) for j0, n in kv_tiles]
        + [proj_small] + [kv_spec_small(j0, n) for j0, n in kv_tiles],
        out_shape=[jax.ShapeDtypeStruct((batch, cols, seq, LANE), F32)]
        + [jax.ShapeDtypeStruct((depth, batch, n, seq, LANE), F32) for _, n in kv_cols]
        + [jax.ShapeDtypeStruct((batch_s, cols, seq_s, LANE), F32)]
        + [jax.ShapeDtypeStruct((depth, batch_s, n, seq_s, LANE), F32) for _, n in kv_cols],
        input_output_aliases=aliases,
        compiler_params=_params(2, vmem),
        name="in_projection",
    )(x, xs, w, *kv_prev, *kv_prev_small)
    return outs[0], tuple(outs[1:1 + n_kv]), outs[1 + n_kv], tuple(outs[2 + n_kv:])


def _sigmoid(x):
    return 1.0 / (1.0 + jnp.exp(-x))


def _gateup_body(x_ref, xs_ref, wg_ref, wu_ref, o_ref, os_ref):
    wg, wu = wg_ref[...].astype(BF16), wu_ref[...].astype(BF16)

    def swiglu(x):
        g = jnp.dot(x, wg, preferred_element_type=F32)
        u = jnp.dot(x, wu, preferred_element_type=F32)
        return (g * _sigmoid(g) * u).astype(BF16)

    o_ref[...] = swiglu(x_ref[...])

    @pl.when(pl.program_id(0) == 0)
    def _():
        os_ref[...] = swiglu(xs_ref[...])


def _gateup(x, xs, wg, wu, layer, tm, tn):
    M, K = x.shape
    Ms = xs.shape[0]
    N = wg.shape[2]
    tm, tn = _pick(M, tm, SUBLANE), _pick(N, tn, LANE)
    nj = N // tn
    vmem = _nbytes((tm, K), BF16) + 2 * (2 * _nbytes((K, tn), F32) + _nbytes((tm, tn), BF16))
    vmem += 4 * _nbytes((tm, tn), F32) + 2 * _nbytes((K, tn), BF16) + 4 * _nbytes((Ms, K), BF16)
    wspec = pl.BlockSpec((None, K, tn), lambda i, j: (layer, 0, j))
    small_out = pl.BlockSpec((Ms, tn), lambda i, j: (0, jnp.where(i == 0, j, nj - 1)))
    return pl.pallas_call(
        _gateup_body,
        grid=(M // tm, nj),
        in_specs=[_resident((tm, K), lambda i, j: (i, 0)), _resident((Ms, K), lambda i, j: (0, 0)),
                  wspec, wspec],
        out_specs=[pl.BlockSpec((tm, tn), lambda i, j: (i, j)), small_out],
        out_shape=[jax.ShapeDtypeStruct((M, N), BF16), jax.ShapeDtypeStruct((Ms, N), BF16)],
        compiler_params=_params(2, vmem),
        name="swiglu_gate_up",
    )(x, xs, wg, wu)


def _ln_body(x_ref, m_ref, g_ref, b_ref, o_ref, ob_ref, *, alpha):
    v = alpha * x_ref[...] + m_ref[...]
    mu = jnp.mean(v, axis=-1, keepdims=True)
    d = v - mu
    var = jnp.mean(d * d, axis=-1, keepdims=True)
    y = d * lax.rsqrt(var + LN_EPS) * g_ref[...] + b_ref[...]
    o_ref[...] = y
    ob_ref[...] = y.astype(BF16)


def _residual_layernorm(x, m, g, b, alpha, name):
    M, D = x.shape
    tm = _pick(M, 256, SUBLANE)
    vmem = 2 * (3 * _nbytes((tm, D), F32) + _nbytes((tm, D), BF16)) + 4 * _nbytes((tm, D), F32)
    row = pl.BlockSpec((tm, D), lambda i: (i, 0))
    vec = pl.BlockSpec((1, D), lambda i: (0, 0))
    return pl.pallas_call(
        functools.partial(_ln_body, alpha=alpha),
        grid=(M // tm,),
        in_specs=[row, row, vec, vec],
        out_specs=[row, row],
        out_shape=[jax.ShapeDtypeStruct((M, D), F32), jax.ShapeDtypeStruct((M, D), BF16)],
        compiler_params=_params(1, vmem),
        name=name,
    )(x, m, g, b)


def _softplus(z):
    return jnp.maximum(z, 0.0) + jnp.log(1.0 + jnp.exp(-jnp.abs(z)))


def _rms(o, g):
    return o * lax.rsqrt(jnp.mean(o * o, axis=-1, keepdims=True) + RMS_EPS) * g


def _lanes(x, n):
    reps = n // LANE
    return x if reps == 1 else jnp.concatenate([x] * reps, axis=1)


def _cumsum_weights():
    j = jnp.arange(LANE)[:, None]
    s = jnp.arange(LANE)[None, :]
    blk = jnp.concatenate([(j > s).astype(BF16), jnp.ones((LANE, LANE), BF16)], axis=1)
    return jnp.concatenate([blk, blk], axis=0)


def _sb_weights(z, valid, run, wcs):
    sp = _softplus(z)
    log_keep = -sp if valid is None else jnp.where(valid, -sp, 0.0)
    log_beta = z - sp
    groups = z.shape[1] // LANE
    ws = [None] * groups
    for g in reversed(range(groups)):
        sl = slice(g * LANE, (g + 1) * LANE)
        lk = log_keep[:, sl]
        hi = lk.astype(BF16)
        lo = (lk - hi.astype(F32)).astype(BF16)
        cs = jnp.dot(jnp.concatenate([hi, lo], axis=1), wcs, preferred_element_type=F32)
        ws[g] = jnp.exp(log_beta[:, sl] + cs[:, :LANE] + run)
        run = run + cs[:, LANE:]
    w = ws[0] if groups == 1 else jnp.concatenate(ws, axis=1)
    if valid is not None:
        w = jnp.where(valid, w, 0.0)
    return w, run


def _softmax_update(s, m_prev, l_prev):
    m_new = jnp.maximum(m_prev, jnp.max(s, axis=1, keepdims=True))
    alpha = jnp.exp(m_prev - m_new)
    p = jnp.exp(s - _lanes(m_new, s.shape[1]))
    return p, alpha, m_new, alpha * l_prev + jnp.sum(p, axis=1, keepdims=True)


def _df_lambda(lq1_ref, lk1_ref, lq2_ref, lk2_ref, lam_init):
    s1 = jnp.sum(lq1_ref[...] * lk1_ref[...], axis=1, keepdims=True)
    s2 = jnp.sum(lq2_ref[...] * lk2_ref[...], axis=1, keepdims=True)
    return jnp.exp(s1) - jnp.exp(s2) + lam_init


def _split_halves(q):
    first = lax.broadcasted_iota(jnp.int32, q.shape, 1) < LANE // 2
    return jnp.concatenate([jnp.where(first, q, 0.0), jnp.where(first, 0.0, q)], axis=0)


def _cast_kv(i, k_ref, v_ref, kb_ref, vb_ref):
    @pl.when(i == 0)
    def _():
        kb_ref[...] = k_ref[...].astype(BF16)
        vb_ref[...] = v_ref[...].astype(BF16)


def _sb_prompt_body(q_ref, k_ref, v_ref, g_ref, wcs_ref, o_ref, kb_ref, vb_ref, acc_ref, run_ref,
                    *, hb, tq, scale):
    i = pl.program_id(2)
    _cast_kv(i, k_ref, v_ref, kb_ref, vb_ref)
    qb = [q_ref[h].astype(BF16) for h in range(hb)]
    wcs = wcs_ref[...]
    acc_ref[...] = jnp.zeros_like(acc_ref)
    run_ref[...] = jnp.zeros_like(run_ref)

    def tile(j, valid):
        ks = pl.multiple_of(j * tq, tq)
        live = None
        for h in range(hb):
            kt = kb_ref[h, pl.ds(ks, tq), :]
            vt = vb_ref[h, pl.ds(ks, tq), :]
            z = lax.dot_general(qb[h], kt, _NT, preferred_element_type=F32) * scale
            w, run = _sb_weights(z, valid, run_ref[h], wcs)
            run_ref[h] = run
            acc_ref[h] += jnp.dot(w.astype(BF16), vt, preferred_element_type=F32)
            top = jnp.max(run)
            live = top if live is None else jnp.maximum(live, top)
        return live

    row = lax.broadcasted_iota(jnp.int32, (tq, tq), 0)
    col = lax.broadcasted_iota(jnp.int32, (tq, tq), 1)
    live = tile(i, col < row)

    def more(c):
        return (c[0] < i) & (c[1] > EXP_UNDERFLOW)

    def body(c):
        return c[0] + 1, tile(i - 1 - c[0], None)

    lax.while_loop(more, body, (jnp.int32(0), live))
    g = g_ref[...]
    for h in range(hb):
        o_ref[:, h * LANE:(h + 1) * LANE] = _rms(acc_ref[h], g).astype(o_ref.dtype)


def _df_prompt_body(q_ref, k_ref, v_ref, g_ref, lq1_ref, lk1_ref, lq2_ref, lk2_ref, o_ref,
                    kb_ref, vb_ref, m_ref, l_ref, acc_ref, *, hb, tq, scale, lam_init):
    i = pl.program_id(2)
    _cast_kv(i, k_ref, v_ref, kb_ref, vb_ref)
    qq = [_split_halves(q_ref[h] * scale).astype(BF16) for h in range(hb)]
    m_ref[...] = jnp.full_like(m_ref, MASK_NEG)
    l_ref[...] = jnp.zeros_like(l_ref)
    acc_ref[...] = jnp.zeros_like(acc_ref)

    def tile(j, valid):
        ks = pl.multiple_of(j * tq, tq)
        for h in range(hb):
            kt = kb_ref[h, pl.ds(ks, tq), :]
            vt = vb_ref[h, pl.ds(ks, tq), :]
            s = lax.dot_general(qq[h], kt, _NT, preferred_element_type=F32)
            if valid is not None:
                s = jnp.where(valid, s, MASK_NEG)
            p, alpha, m_ref[h], l_ref[h] = _softmax_update(s, m_ref[h], l_ref[h])
            acc_ref[h] = acc_ref[h] * alpha + jnp.dot(p.astype(BF16), vt, preferred_element_type=F32)

    row = lax.broadcasted_iota(jnp.int32, (2 * tq, tq), 0)
    col = lax.broadcasted_iota(jnp.int32, (2 * tq, tq), 1)
    row = jnp.where(row >= tq, row - tq, row)
    tile(i, col <= row)

    def body(jj, c):
        tile(jj, None)
        return c

    lax.fori_loop(0, i, body, 0)
    g = g_ref[...]
    lam = _df_lambda(lq1_ref, lk1_ref, lq2_ref, lk2_ref, lam_init)
    for h in range(hb):
        o = acc_ref[h] / l_ref[h]
        o = o[:tq] - lam * o[tq:]
        o_ref[:, h * LANE:(h + 1) * LANE] = (_rms(o, g) * (1.0 - lam_init)).astype(o_ref.dtype)


def _prompt_attention(kind, proj, kv_k, kv_v, layer, col0, norm_g, extra, tq, hb):
    batch, _, seq, _ = proj.shape
    heads = kv_k.shape[2]
    tq = _pick(seq, tq, LANE)
    nq = seq // tq
    hb = _heads_per_step(heads, col0, hb)
    qspec = pl.BlockSpec((None, hb, tq, LANE), lambda b, h, i: (b, col0 // hb + h, i, 0))
    kvspec = pl.BlockSpec((None, None, hb, seq, LANE), lambda b, h, i: (layer, b, h, 0, 0))
    const = lambda shape: pl.BlockSpec(shape, lambda b, h, i: (0,) * len(shape))
    kv_scratch = [pltpu.VMEM((hb, seq, LANE), BF16), pltpu.VMEM((hb, seq, LANE), BF16)]
    vmem = hb * (4 * _nbytes((seq, LANE), F32) + 4 * _nbytes((tq, LANE), F32) + 2 * _nbytes((seq, LANE), BF16))
    vmem += hb * 16 * _nbytes((2 * tq, tq), F32)
    if kind == "sb":
        body = functools.partial(_sb_prompt_body, hb=hb, tq=tq, scale=LANE ** -0.5)
        ins = [proj, kv_k, kv_v, norm_g, _cumsum_weights()]
        specs = [qspec, kvspec, kvspec, const((1, LANE)), const((2 * LANE, 2 * LANE))]
        scratch = kv_scratch + [pltpu.VMEM((hb, tq, LANE), F32)] * 2
    else:
        lam_init, lq1, lk1, lq2, lk2 = extra
        scale = (LANE // 2) ** -0.5
        assert math.frexp(scale)[0] == 0.5
        body = functools.partial(_df_prompt_body, hb=hb, tq=tq, scale=scale, lam_init=lam_init)
        ins = [proj, kv_k, kv_v, norm_g, lq1, lk1, lq2, lk2]
        specs = [qspec, kvspec, kvspec, const((1, LANE))] + [const((1, LANE // 2))] * 4
        scratch = kv_scratch + [pltpu.VMEM((hb, 2 * tq, LANE), F32)] * 3
    return pl.pallas_call(
        body,
        grid=(batch, heads // hb, nq),
        in_specs=specs,
        out_specs=pl.BlockSpec((tq, hb * LANE), lambda b, h, i: (b * nq + i, h)),
        out_shape=jax.ShapeDtypeStruct((batch * seq, heads * LANE), BF16),
        scratch_shapes=scratch,
        compiler_params=_params(3, vmem),
        name=kind + "_attention_prompt",
    )(*ins)


def _hgrn_body(q_ref, f_ref, i_ref, gate_ref, lbp_ref, gn_ref, lmat_ref, s0_ref, o_ref, s_ref,
               st_ref, qa_ref, key_ref, bl_ref, oraw_ref, *, hb, rows, layer, depth, t_valid, nt):
    t = pl.program_id(2)

    @pl.when(t == 0)
    def _():
        for h in range(hb):
            st_ref[h] = s0_ref[h].T

    lmat = lmat_ref[...]
    for h in range(hb):
        lbp = lbp_ref[:, h * LANE:(h + 1) * LANE]
        prow = [lbp[r:r + 1] for r in range(depth)]
        mx = functools.reduce(jnp.maximum, prow)
        es = [jnp.exp(r - mx) for r in prow]
        lb = sum(es[1:layer + 1], jnp.zeros_like(mx)) / sum(es[1:], es[0])
        p = f_ref[h]
        log_f = -_softplus(-p) + jnp.log1p(lb * jnp.exp(-p))
        key = (1.0 - lb) * (1.0 / (1.0 + jnp.exp(p)))
        if t_valid is not None:
            ok = (t * rows + lax.broadcasted_iota(jnp.int32, p.shape, 0)) < t_valid
            log_f = jnp.where(ok, log_f, 0.0)
            key = jnp.where(ok, key, 0.0)
        hi = log_f.astype(BF16)
        r1 = log_f - hi.astype(F32)
        mid = r1.astype(BF16)
        lo = (r1 - mid.astype(F32)).astype(BF16)
        bl_ref[h] = jnp.dot(lmat, jnp.concatenate([hi, mid, lo], axis=0), preferred_element_type=F32)
        qv = q_ref[h]
        qa_ref[h] = qv * _sigmoid(qv)
        key_ref[h] = key

    rowi = lax.broadcasted_iota(jnp.int32, (HG_CHUNK, LANE), 0)

    def chunk(c, carry):
        r0 = pl.multiple_of(c * HG_CHUNK, HG_CHUNK)
        for h in range(hb):
            blc = bl_ref[h, pl.ds(r0, HG_CHUNK), :]
            qac = qa_ref[h, pl.ds(r0, HG_CHUNK), :]
            kc = key_ref[h, pl.ds(r0, HG_CHUNK), :]
            vc = i_ref[h, pl.ds(r0, HG_CHUNK), :]
            blast = blc[HG_CHUNK - 1:HG_CHUNK]
            st = st_ref[h]
            o = lax.dot_general((qac * jnp.exp(blc)).astype(BF16), st.astype(BF16), _NT,
                                preferred_element_type=F32)
            for s in range(HG_CHUNK):
                e = jnp.exp(jnp.where(rowi >= s, blc - blc[s:s + 1], MASK_NEG))
                cv = jnp.sum(qac * e * kc[s:s + 1], axis=1, keepdims=True)
                o = o + cv * vc[s:s + 1]
            oraw_ref[h, pl.ds(r0, HG_CHUNK), :] = o
            kt = kc * jnp.exp(blast - blc)
            upd = lax.dot_general(vc.astype(BF16), kt.astype(BF16), _TN, preferred_element_type=F32)
            st_ref[h] = st * jnp.exp(blast) + upd
        return carry

    lax.fori_loop(0, rows // HG_CHUNK, chunk, 0, unroll=min(4, rows // HG_CHUNK))
    gn = gn_ref[...]
    for h in range(hb):
        gate = gate_ref[h]
        o_ref[:, h * LANE:(h + 1) * LANE] = (_rms(oraw_ref[h], gn) * (gate * _sigmoid(gate))).astype(o_ref.dtype)

    @pl.when(t == nt - 1)
    def _():
        for h in range(hb):
            s_ref[h] = st_ref[h].T


def _hgrn(proj, heads, col0, hg_lb, norm_g, s0, layer, t_valid, hb):
    batch, _, seq, _ = proj.shape
    rows = _pick(seq, 256, HG_CHUNK)
    nt = seq // rows
    depth = hg_lb.shape[0]
    hb = _heads_per_step(heads, col0, hb)
    r = jnp.arange(rows)
    lmat = ((r[:, None] // HG_CHUNK == r[None, :] // HG_CHUNK) & (r[None, :] <= r[:, None])).astype(BF16)
    lmat = jnp.concatenate([lmat] * 3, axis=1)

    def col(k):
        return pl.BlockSpec((None, hb, rows, LANE), lambda b, h, t: (b, (col0 + k * heads) // hb + h, t, 0))

    state = pl.BlockSpec((None, hb, LANE, LANE), lambda b, h, t: (b, h, 0, 0))
    vmem = hb * (2 * (5 * _nbytes((rows, LANE), F32) + 4 * _nbytes((LANE, LANE), F32)) + 16 * _nbytes((rows, LANE), F32))
    vmem += 2 * _nbytes((rows, 3 * rows), BF16)
    return pl.pallas_call(
        functools.partial(_hgrn_body, hb=hb, rows=rows, layer=layer, depth=depth,
                          t_valid=None if t_valid == seq else t_valid, nt=nt),
        grid=(batch, heads // hb, nt),
        in_specs=[col(0), col(1), col(2), col(3),
                  pl.BlockSpec((depth, hb * LANE), lambda b, h, t: (0, h)),
                  pl.BlockSpec((1, LANE), lambda b, h, t: (0, 0)),
                  pl.BlockSpec((rows, 3 * rows), lambda b, h, t: (0, 0)),
                  state],
        out_specs=[pl.BlockSpec((rows, hb * LANE), lambda b, h, t: (b * nt + t, h)), state],
        out_shape=[jax.ShapeDtypeStruct((batch * seq, heads * LANE), BF16),
                   jax.ShapeDtypeStruct(s0.shape, F32)],
        scratch_shapes=[pltpu.VMEM((hb, LANE, LANE), F32)] + [pltpu.VMEM((hb, rows, LANE), F32)] * 4,
        compiler_params=_params(3, vmem),
        name="hgrn2",
    )(proj, proj, proj, proj, hg_lb, norm_g, lmat, s0)


def _head_scores(qb, k_refs, heads, rows_per_head):
    out = []
    for h in range(heads):
        kh = [r[h].astype(BF16) for r in k_refs]
        kh = kh[0] if len(kh) == 1 else jnp.concatenate(kh, axis=0)
        out.append(lax.dot_general(qb[h * rows_per_head:(h + 1) * rows_per_head], kh, _NT,
                                   preferred_element_type=F32))
    return jnp.concatenate(out, axis=0)


def _head_values(w, v_refs, heads, rows_per_head):
    out = []
    for h in range(heads):
        vh = [r[h].astype(BF16) for r in v_refs]
        vh = vh[0] if len(vh) == 1 else jnp.concatenate(vh, axis=0)
        out.append(jnp.dot(w[h * rows_per_head:(h + 1) * rows_per_head].astype(BF16), vh,
                           preferred_element_type=F32))
    return jnp.concatenate(out, axis=0)


def _write_heads(o_ref, head_fn, heads):
    pad = jnp.zeros((o_ref.shape[0] - Q_SLOTS, LANE), F32)
    for h in range(heads):
        o_ref[:, h * LANE:(h + 1) * LANE] = jnp.concatenate([head_fn(h), pad], axis=0).astype(o_ref.dtype)


def _sb_decode_head_body(pt_ref, q_ref, kn_ref, vn_ref, *rest, group, heads, scale):
    k_refs, v_refs = rest[:group], rest[group:2 * group]
    wcs_ref, acc_ref, run_ref, live_ref = rest[2 * group:]
    p = pl.program_id(1)
    qb = q_ref[...]
    wcs = wcs_ref[...]

    @pl.when(p == 0)
    def _():
        z = _head_scores(qb, [kn_ref], heads, Q_SLOTS) * scale
        tok = lax.broadcasted_iota(jnp.int32, z.shape, 0) & (Q_SLOTS - 1)
        col = lax.broadcasted_iota(jnp.int32, z.shape, 1)
        w, run = _sb_weights(z, col < tok, jnp.zeros(run_ref.shape, F32), wcs)
        run_ref[...] = run
        acc_ref[...] = _head_values(w, [vn_ref], heads, Q_SLOTS)

    @pl.when(p == 1)
    def _():
        z = _head_scores(qb, k_refs, heads, Q_SLOTS) * scale
        w, run = _sb_weights(z, None, run_ref[...], wcs)
        run_ref[...] = run
        acc_ref[...] += _head_values(w, v_refs, heads, Q_SLOTS)
        live_ref[...] = jnp.full(live_ref.shape, 1, jnp.int32) * (jnp.max(run) > EXP_UNDERFLOW).astype(jnp.int32)


def _sb_decode_finish_body(acc_ref, g_ref, o_ref, *, heads):
    g = g_ref[...]
    _write_heads(o_ref, lambda h: _rms(acc_ref[h * Q_SLOTS:(h + 1) * Q_SLOTS, :], g), heads)


def _sb_decode_tail_body(pt_ref, live_ref, q_ref, *rest, group, heads, scale):
    k_refs, v_refs = rest[:group], rest[group:2 * group]
    g_ref, wcs_ref, acc0_ref, run0_ref, o_ref, acc_ref, run_ref = rest[2 * group:]
    b, s = pl.program_id(0), pl.program_id(1)

    @pl.when(s == 0)
    def _():
        acc_ref[...] = acc0_ref[...]
        run_ref[...] = run0_ref[...]

    @pl.when(live_ref[b] > 0)
    def _():
        z = _head_scores(q_ref[...], k_refs, heads, Q_SLOTS) * scale
        w, run = _sb_weights(z, None, run_ref[...], wcs_ref[...])
        run_ref[...] = run
        acc_ref[...] += _head_values(w, v_refs, heads, Q_SLOTS)

    @pl.when(s == pl.num_programs(1) - 1)
    def _():
        g = g_ref[...]
        _write_heads(o_ref, lambda h: _rms(acc_ref[h * Q_SLOTS:(h + 1) * Q_SLOTS, :], g), heads)


def _df_decode_body(pt_ref, q_ref, kn_ref, vn_ref, *rest, group, heads, scale, lam_init):
    k_refs, v_refs = rest[:group], rest[group:2 * group]
    g_ref, lq1_ref, lk1_ref, lq2_ref, lk2_ref, o_ref, m_ref, l_ref, acc_ref = rest[2 * group:]
    p = pl.program_id(1)
    qb = q_ref[...]
    rph = 2 * Q_SLOTS

    def step(s, v_parts, first):
        m_prev = jnp.full(m_ref.shape, MASK_NEG, F32) if first else m_ref[...]
        l_prev = jnp.zeros(l_ref.shape, F32) if first else l_ref[...]
        pr, alpha, m_ref[...], l_ref[...] = _softmax_update(s, m_prev, l_prev)
        pv = _head_values(pr, v_parts, heads, rph)
        acc_ref[...] = pv if first else acc_ref[...] * alpha + pv

    @pl.when(p == 0)
    def _():
        s = _head_scores(qb, [kn_ref], heads, rph) * scale
        tok = lax.broadcasted_iota(jnp.int32, s.shape, 0) & (Q_SLOTS - 1)
        col = lax.broadcasted_iota(jnp.int32, s.shape, 1)
        step(jnp.where(col <= tok, s, MASK_NEG), [vn_ref], True)

    @pl.when(p > 0)
    def _():
        step(_head_scores(qb, k_refs, heads, rph) * scale, v_refs, False)

    @pl.when(p == pl.num_programs(1) - 1)
    def _():
        g = g_ref[...]
        lam = _df_lambda(lq1_ref, lk1_ref, lq2_ref, lk2_ref, lam_init)

        def head(h):
            r1 = slice(h * rph, h * rph + Q_SLOTS)
            r2 = slice(h * rph + Q_SLOTS, (h + 1) * rph)
            o = acc_ref[r1, :] / l_ref[r1, :] - lam * (acc_ref[r2, :] / l_ref[r2, :])
            return _rms(o, g) * (1.0 - lam_init)

        _write_heads(o_ref, head, heads)


def _decode_inputs(q, k_new, v_new, page, split):
    B, heads, tq, _ = q.shape
    pad = ((0, 0), (0, 0), (0, page - tq), (0, 0))
    k_new, v_new = jnp.pad(k_new, pad), jnp.pad(v_new, pad)
    q = jnp.pad(q, ((0, 0), (0, 0), (0, Q_SLOTS - tq), (0, 0)))
    if split:
        first = jnp.arange(LANE) < LANE // 2
        q = jnp.stack([jnp.where(first, q, 0.0), jnp.where(first, 0.0, q)], axis=2)
    return q.reshape(B, -1, LANE).astype(BF16), k_new, v_new


def _page_specs(layer, heads, page, group, page_of):
    def spec(gi):
        return pl.BlockSpec((None, None, heads, page, LANE),
                            lambda b, s, *pre: (layer, page_of(gi, b, s, *pre), 0, 0, 0))
    return [spec(gi) for gi in range(group)]


def _decode_vmem(heads, page, group, qrows):
    vmem = 2 * (2 * group + 2) * _nbytes((heads, page, LANE), F32)
    return vmem + 4 * _nbytes((heads, group * page, LANE), BF16) + 24 * _nbytes((qrows, group * page), F32)


def _sb_decode_attention(q, k_new, v_new, pool_k, pool_v, page_table, layer, norm_g, out_rows, group):
    B, heads, _, _ = q.shape
    page = pool_k.shape[3]
    n_pages = page_table.shape[1]
    group = _pick(n_pages // 2, group, 1)
    n_groups = n_pages // group
    qb, k_new, v_new = _decode_inputs(q, k_new, v_new, page, False)
    qrows = heads * Q_SLOTS
    wcs = _cumsum_weights()
    scale = LANE ** -0.5
    vmem = _decode_vmem(heads, page, group, qrows)
    per_b = lambda shape: pl.BlockSpec((None,) + shape, lambda b, s, *pre: (b,) + (0,) * len(shape))
    const = lambda shape: pl.BlockSpec(shape, lambda b, s, *pre: (0,) * len(shape))
    state = jax.ShapeDtypeStruct((B, qrows, LANE), F32)

    last = _page_specs(layer, heads, page, group, lambda gi, b, s, pt: pt[b, (n_groups - 1) * group + gi])
    acc, run, live = pl.pallas_call(
        functools.partial(_sb_decode_head_body, group=group, heads=heads, scale=scale),
        grid_spec=pltpu.PrefetchScalarGridSpec(
            num_scalar_prefetch=1,
            grid=(B, 2),
            in_specs=[per_b((qrows, LANE)), per_b((heads, page, LANE)), per_b((heads, page, LANE))]
            + last + last + [const((2 * LANE, 2 * LANE))],
            out_specs=[per_b((qrows, LANE)), per_b((qrows, LANE)), per_b((SUBLANE, LANE))],
        ),
        out_shape=[state, state, jax.ShapeDtypeStruct((B, SUBLANE, LANE), jnp.int32)],
        compiler_params=_params(2, vmem),
        name="sb_attention_decode_head",
    )(page_table, qb, k_new, v_new, *([pool_k] * group), *([pool_v] * group), wcs)
    live = live[:, 0, 0]

    def earlier(gi, b, s, pt, lv):
        grp = jnp.where(lv[b] > 0, n_groups - 2 - s, 0)
        return pt[b, grp * group + gi]

    pages = _page_specs(layer, heads, page, group, earlier)
    out_shape = jax.ShapeDtypeStruct((B, out_rows, heads * LANE), BF16)

    def rest():
        return pl.pallas_call(
            functools.partial(_sb_decode_tail_body, group=group, heads=heads, scale=scale),
            grid_spec=pltpu.PrefetchScalarGridSpec(
                num_scalar_prefetch=2,
                grid=(B, n_groups - 1),
                in_specs=[per_b((qrows, LANE))] + pages + pages
                + [const((1, LANE)), const((2 * LANE, 2 * LANE)), per_b((qrows, LANE)), per_b((qrows, LANE))],
                out_specs=per_b((out_rows, heads * LANE)),
                scratch_shapes=[pltpu.VMEM((qrows, LANE), F32), pltpu.VMEM((qrows, LANE), F32)],
            ),
            out_shape=out_shape,
            compiler_params=_params(2, vmem),
            name="sb_attention_decode_tail",
        )(page_table, live, qb, *([pool_k] * group), *([pool_v] * group), norm_g, wcs, acc, run)

    def finish():
        return pl.pallas_call(
            functools.partial(_sb_decode_finish_body, heads=heads),
            grid=(B,),
            in_specs=[pl.BlockSpec((None, qrows, LANE), lambda b: (b, 0, 0)), pl.BlockSpec((1, LANE), lambda b: (0, 0))],
            out_specs=pl.BlockSpec((None, out_rows, heads * LANE), lambda b: (b, 0, 0)),
            out_shape=out_shape,
            compiler_params=_params(1, 0),
            name="sb_attention_decode_finish",
        )(acc, norm_g)

    return lax.cond(jnp.any(live > 0), rest, finish)


def _df_decode_attention(q, k_new, v_new, pool_k, pool_v, page_table, layer, norm_g, extra, out_rows, group):
    B, heads, _, _ = q.shape
    page = pool_k.shape[3]
    n_pages = page_table.shape[1]
    group = _pick(n_pages, group, 1)
    n_groups = n_pages // group
    qb, k_new, v_new = _decode_inputs(q, k_new, v_new, page, True)
    qrows = heads * 2 * Q_SLOTS
    lam_init, lq1, lk1, lq2, lk2 = extra
    per_b = lambda shape: pl.BlockSpec((None,) + shape, lambda b, p, pt: (b,) + (0,) * len(shape))
    const = lambda shape: pl.BlockSpec(shape, lambda b, p, pt: (0,) * len(shape))
    pages = _page_specs(layer, heads, page, group,
                        lambda gi, b, p, pt: pt[b, (n_groups - jnp.maximum(p, 1)) * group + gi])
    return pl.pallas_call(
        functools.partial(_df_decode_body, group=group, heads=heads, scale=(LANE // 2) ** -0.5,
                          lam_init=lam_init),
        grid_spec=pltpu.PrefetchScalarGridSpec(
            num_scalar_prefetch=1,
            grid=(B, n_groups + 1),
            in_specs=[per_b((qrows, LANE)), per_b((heads, page, LANE)), per_b((heads, page, LANE))]
            + pages + pages + [const((1, LANE))] + [const((1, LANE // 2))] * 4,
            out_specs=per_b((out_rows, heads * LANE)),
            scratch_shapes=[pltpu.VMEM((qrows, LANE), F32)] * 3,
        ),
        out_shape=jax.ShapeDtypeStruct((B, out_rows, heads * LANE), BF16),
        compiler_params=_params(2, _decode_vmem(heads, page, group, qrows)),
        name="df_attention_decode",
    )(page_table, qb, k_new, v_new, *([pool_k] * group), *([pool_v] * group), norm_g, lq1, lk1, lq2, lk2)


def _mixers(proj, kv, batch, seq, t_valid, layer, w, s_hg_past, cache, page_table, dims):
    h_sb, h_hg, h_df = dims
    lam_init = 0.8 - 0.6 * math.exp(-0.3 * layer)
    c_hg, c_df = 3 * h_sb, 3 * h_sb + 4 * h_hg
    row = lambda name: w[name][layer][None, :]
    df_extra = (lam_init, row("lq1"), row("lk1"), row("lq2"), row("lk2"))
    if cache is None:
        o_sb = _prompt_attention("sb", proj, kv[0], kv[1], layer, 0, row("sb_g"), None, 256, 4)
        o_df = _prompt_attention("df", proj, kv[2], kv[3], layer, c_df, row("df_g"), df_extra, 256, 4)
    else:
        new = lambda a: a[layer, :, :, :t_valid]
        o_sb = _sb_decode_attention(proj[:, :h_sb, :t_valid], new(kv[0]), new(kv[1]), cache[0], cache[1],
                                    page_table, layer, row("sb_g"), seq, 8).reshape(batch * seq, -1)
        o_df = _df_decode_attention(proj[:, c_df:c_df + h_df, :t_valid], new(kv[2]), new(kv[3]), cache[2],
                                    cache[3], page_table, layer, row("df_g"), df_extra, seq, 8
                                    ).reshape(batch * seq, -1)
    o_hg, s_hg = _hgrn(proj, h_hg, c_hg, w["hg_lb"], row("hg_g"), s_hg_past, layer, t_valid, 4)
    return [o_sb, o_hg, o_df], s_hg


def kernel(x_prompt, x_sample, cache_sb_k, cache_sb_v, cache_df_k, cache_df_v, state_hg, page_table, w_in, w_out, sb_norm_g, hg_lb, hg_norm_g, df_lq1, df_lk1, df_lq2, df_lk2, df_norm_g, ln1_g, ln1_b, w_gate, w_up, w_down, ln2_g, ln2_b):
    depth = w_in.shape[0]
    bp, seq, d_model = x_prompt.shape
    bs, dec_seq, _ = x_sample.shape
    h_sb, h_hg, h_df = dims = (cache_sb_k.shape[3], state_hg.shape[2], cache_df_k.shape[3])
    assert dec_seq <= Q_SLOTS and cache_sb_k.shape[4] == LANE and state_hg.shape[3:] == (LANE, LANE)
    dec_rows = HG_CHUNK
    w = {"down": w_down.astype(BF16),
         "sb_g": sb_norm_g, "hg_g": hg_norm_g, "df_g": df_norm_g, "hg_lb": hg_lb,
         "lq1": df_lq1, "lk1": df_lk1, "lq2": df_lq2, "lk2": df_lk2}
    alpha = (2 * depth) ** 0.25
    c_df = 3 * h_sb + 4 * h_hg
    kv_cols = ((h_sb, h_sb), (2 * h_sb, h_sb), (c_df + h_df, h_df), (c_df + 2 * h_df, h_df))

    xp = x_prompt.reshape(bp * seq, d_model)
    xs = jnp.pad(x_sample, ((0, 0), (0, dec_rows - dec_seq), (0, 0))).reshape(bs * dec_rows, d_model)
    xpb, xsb = xp.astype(BF16), xs.astype(BF16)
    cache = tuple(jnp.transpose(c, (0, 1, 3, 2, 4)) for c in (cache_sb_k, cache_sb_v, cache_df_k, cache_df_v))
    s_zero = jnp.zeros((bp,) + state_hg.shape[2:], state_hg.dtype)
    kv_p, kv_s, st_p, st_s = (), (), [], []
    for layer in range(depth):
        vec = lambda a: a[layer][None, :]
        proj_p, kv_p, proj_s, kv_s = _in_projection(xpb, xsb, w_in, layer, (bp, seq), (bs, dec_rows), kv_cols,
                                                    kv_p, kv_s)
        mix_p, s = _mixers(proj_p, kv_p, bp, seq, seq, layer, w, s_zero, None, None, dims)
        st_p.append(s)
        mix_s, s = _mixers(proj_s, kv_s, bs, dec_rows, dec_seq, layer, w, state_hg[layer], cache, page_table,
                           dims)
        st_s.append(s)
        m_p, m_s = _out_projection(mix_p, mix_s, w_out, layer, 2048, 512)
        hp, hpb = _residual_layernorm(xp, m_p, vec(ln1_g), vec(ln1_b), alpha, "layernorm_mixer")
        hs, hsb = _residual_layernorm(xs, m_s, vec(ln1_g), vec(ln1_b), alpha, "layernorm_mixer")
        hid_p, hid_s = _gateup(hpb, hsb, w_gate, w_up, layer, 2048, 256)
        f_p, f_s = _matmul(hid_p, hid_s, w["down"], layer, 1024, 256, "swiglu_down")
        xp, xpb = _residual_layernorm(hp, f_p, vec(ln2_g), vec(ln2_b), alpha, "layernorm_ffn")
        xs, xsb = _residual_layernorm(hs, f_s, vec(ln2_g), vec(ln2_b), alpha, "layernorm_ffn")
    y_prompt = xp.reshape(bp, seq, d_model)
    y_sample = xs.reshape(bs, dec_rows, d_model)[:, :dec_seq]
    tokens_major = lambda a, t: jnp.transpose(a[:, :, :, :t], (0, 1, 3, 2, 4))
    return (y_prompt, y_sample,
            *(tokens_major(a, seq) for a in kv_p), jnp.stack(st_p),
            *(tokens_major(a, dec_seq) for a in kv_s), jnp.stack(st_s))
```
